```python
import jax, jax.numpy as jnp
from jax import lax
import numpy as np

D_MODEL = 1024
BATCH = 8
SEQ = 4096
DEPTH = 1

GRID_W = 64
CTX_LEN = 256
MIX_W = D_MODEL
F_W = MIX_W // 2
F_GROUPS = 4
F_GROUP_W = F_W // F_GROUPS
M_W = MIX_W - F_W
M_HEADS = 4
M_HEAD_DIM = M_W // M_HEADS
N_DIRS = 2
N_GATES = N_DIRS * 2 * M_HEADS
MLSTM_END = F_W + 2 * M_W + N_GATES
IN_W = MLSTM_END + M_W
CHUNK = 128
CONV_W = 3
D_FF = 5 * D_MODEL // 2
NORM_EPS = 1e-6

kernel_name = 'hybrid_fourier_mlstm_convffn_dit'


def _rmsnorm(x, w):
    xf = x.astype(jnp.float32)
    y = xf * lax.rsqrt(jnp.mean(xf * xf, axis=-1, keepdims=True) + NORM_EPS)
    return (y * w.astype(jnp.float32)).astype(x.dtype)


def _adaln(cond, w_ada, b_ada):
    mod = jax.nn.silu(cond) @ w_ada + b_ada
    return [m[..., None, :] for m in jnp.split(mod, 6, axis=-1)]


def _modnorm(x, w, shift, scale):
    return _rmsnorm(x, w) * (1 + scale) + shift


def _dwconv1d(x, w, b):
    y = lax.conv_general_dilated(x, w.astype(x.dtype)[:, None, :], (1,), 'SAME',
                                 dimension_numbers=('NWC', 'WIO', 'NWC'),
                                 feature_group_count=x.shape[-1])
    return y + b


def _dwconv_grid(x, w, b):
    bsz, length, ch = x.shape
    rows = length // GRID_W
    xg = x.reshape(bsz, rows, GRID_W, ch)
    y = lax.conv_general_dilated(xg, w.astype(x.dtype)[:, :, None, :], (1, 1), 'SAME',
                                 dimension_numbers=('NHWC', 'HWIO', 'NHWC'),
                                 feature_group_count=ch)
    return y.reshape(bsz, length, ch) + b


def _fourier_mix(u):
    bsz, length, _ = u.shape
    ug = u.astype(jnp.float32).reshape(bsz, length, F_GROUPS, F_GROUP_W)
    y = jnp.fft.fftn(ug, axes=(1, 3), norm='ortho').real
    return y.reshape(bsz, length, F_W).astype(u.dtype)


def _zero_state(bsz):
    return (jnp.zeros((bsz, M_HEADS, M_HEAD_DIM, M_HEAD_DIM), jnp.float32),
            jnp.zeros((bsz, M_HEADS, M_HEAD_DIM), jnp.float32),
            jnp.zeros((bsz, M_HEADS), jnp.float32))


def _state_update(state, k, v, li, lf):
    c_mat, n_vec, m = state
    b = jnp.cumsum(lf, axis=-1)
    b_end = b[..., -1]
    g = b_end[..., None] - b + li
    m_new = jnp.maximum(b_end + m, jnp.max(g, axis=-1))
    w_old = jnp.exp(b_end + m - m_new)
    w_tok = jnp.exp(g - m_new[..., None])
    c_new = w_old[..., None, None] * c_mat + jnp.einsum('bhj,bhjd,bhje->bhde', w_tok, k, v)
    n_new = w_old[..., None] * n_vec + jnp.einsum('bhj,bhjd->bhd', w_tok, k)
    return (c_new, n_new, m_new)


def _mlstm_chunk(state, inp):
    c_mat, n_vec, m = state
    q, k, v, li, lf = inp
    length = q.shape[-2]
    b = jnp.cumsum(lf, axis=-1)
    lower = jnp.tril(jnp.ones((length, length), dtype=bool))
    dmat = jnp.where(lower, b[..., :, None] - b[..., None, :] + li[..., None, :], -jnp.inf)
    inter = b + m[..., None]
    m_t = jnp.maximum(inter, jnp.max(dmat, axis=-1))
    w_inter = jnp.exp(inter - m_t)
    s = jnp.einsum('bhld,bhjd->bhlj', q, k) * jnp.exp(dmat - m_t[..., None])
    num = w_inter[..., None] * jnp.einsum('bhld,bhde->bhle', q, c_mat) + jnp.einsum('bhlj,bhje->bhle', s, v)
    den = w_inter * jnp.einsum('bhld,bhd->bhl', q, n_vec) + jnp.sum(s, axis=-1)
    h = num / jnp.maximum(jnp.abs(den), jnp.exp(-m_t))[..., None]
    return _state_update(state, k, v, li, lf), h


def _mlstm_scan(q, k, v, li, lf, state0):
    bsz, heads, length, dh = q.shape
    nc = length // CHUNK

    def chunks(t):
        return jnp.moveaxis(t.reshape(bsz, heads, nc, CHUNK, *t.shape[3:]), 2, 0)

    state, h = lax.scan(_mlstm_chunk, state0, (chunks(q), chunks(k), chunks(v), chunks(li), chunks(lf)))
    return jnp.moveaxis(h, 0, 2).reshape(bsz, heads, length, dh), state


def _mlstm_inputs(p, mconv_w, mconv_b, w_k, b_gate):
    bsz, length, _ = p.shape
    xm, v, g = jnp.split(p, [M_W, 2 * M_W], axis=-1)
    a = jax.nn.silu(_dwconv1d(xm, mconv_w, mconv_b))
    ah = a.reshape(bsz, length, M_HEADS, M_HEAD_DIM)
    k = (jnp.einsum('blhd,hde->bhle', ah, w_k) * M_HEAD_DIM ** -0.5).astype(jnp.float32)
    vh = v.reshape(bsz, length, M_HEADS, M_HEAD_DIM).transpose(0, 2, 1, 3).astype(jnp.float32)
    g = (g + b_gate).astype(jnp.float32).reshape(bsz, length, N_DIRS, 2, M_HEADS)
    g = jnp.transpose(g, (2, 3, 0, 4, 1))
    li = g[:, 0]
    lf = jax.nn.log_sigmoid(g[:, 1])
    return a, k, vh, li, lf


def _mlstm_head_out(h, a, z, mnorm_w, m_skip):
    bsz, heads, length, dh = h.shape
    mu = jnp.mean(h, axis=-1, keepdims=True)
    var = jnp.mean(jnp.square(h - mu), axis=-1, keepdims=True)
    hn = ((h - mu) * lax.rsqrt(var + NORM_EPS)).transpose(0, 2, 1, 3).reshape(bsz, length, M_W)
    hn = (hn * mnorm_w.astype(jnp.float32)).astype(a.dtype)
    return (hn + m_skip * a) * jax.nn.silu(z)


def _token_mixer(h, w_in, w_out, mconv_w, mconv_b, w_q, w_k, b_gate, mnorm_w, m_skip, st_f, st_b):
    bsz, length, _ = h.shape
    proj = h @ w_in
    a, k, v, li, lf = _mlstm_inputs(proj[..., F_W:MLSTM_END], mconv_w, mconv_b, w_k, b_gate)
    q = jnp.einsum('blhd,hde->bhle', a.reshape(bsz, length, M_HEADS, M_HEAD_DIM), w_q).astype(jnp.float32)
    h_f, fin_f = _mlstm_scan(q, k, v, li[0], lf[0], st_f)
    h_b, fin_b = _mlstm_scan(jnp.flip(q, 2), jnp.flip(k, 2), jnp.flip(v, 2),
                             jnp.flip(li[1], -1), jnp.flip(lf[1], -1), st_b)
    y_m = _mlstm_head_out(h_f + jnp.flip(h_b, 2), a, proj[..., MLSTM_END:], mnorm_w, m_skip)
    y = jnp.concatenate([_fourier_mix(proj[..., :F_W]), y_m], axis=-1) @ w_out
    return y, fin_f, fin_b


def _context_states(p, mconv_w, mconv_b, w_k, b_gate):
    _, k, v, li, lf = _mlstm_inputs(p, mconv_w, mconv_b, w_k, b_gate)
    zero = _zero_state(p.shape[0])
    st_f = _state_update(zero, k, v, li[0], lf[0])
    st_b = _state_update(zero, jnp.flip(k, 2), jnp.flip(v, 2), jnp.flip(li[1], -1), jnp.flip(lf[1], -1))
    return st_f, st_b


def _conv_ffn(h, w_up, b_up, fconv_w, fconv_b, w_down, b_down, on_grid):
    u = h @ w_up + b_up
    u = _dwconv_grid(u, fconv_w, fconv_b) if on_grid else _dwconv1d(u, fconv_w[1], fconv_b)
    val, gate = jnp.split(u, 2, axis=-1)
    return (val * jax.nn.silu(gate)) @ w_down + b_down


def setup_inputs(seed: int = 0) -> dict:
    key = jax.random.key(seed)
    ks = jax.random.split(key, 28)
    f32 = jnp.float32

    def nrm(k, shape, scale):
        return jax.random.normal(k, shape, f32) * scale

    L = DEPTH
    ig = nrm(ks[12], (L, N_DIRS, 1, M_HEADS), 0.1)
    fg = jnp.linspace(3.0, 6.0, M_HEADS, dtype=f32) + nrm(ks[13], (L, N_DIRS, 1, M_HEADS), 0.1)
    return {
        'x': nrm(ks[0], (BATCH, SEQ, D_MODEL), 1.0),
        'c': nrm(ks[1], (BATCH, D_MODEL), 1.0),
        'ctx': nrm(ks[2], (BATCH, CTX_LEN, D_MODEL), 1.0),
        'c_ctx': nrm(ks[3], (D_MODEL,), 1.0),
        'w_ada': nrm(ks[4], (L, D_MODEL, 6 * D_MODEL), 0.5 * D_MODEL ** -0.5),
        'b_ada': nrm(ks[5], (L, 6 * D_MODEL), 0.02),
        'norm1_w': 1.0 + nrm(ks[6], (L, D_MODEL), 0.02),
        'w_in': nrm(ks[7], (L, D_MODEL, IN_W), D_MODEL ** -0.5),
        'mconv_w': nrm(ks[8], (L, CONV_W, M_W), CONV_W ** -0.5),
        'mconv_b': nrm(ks[9], (L, M_W), 0.02),
        'w_q': nrm(ks[10], (L, M_HEADS, M_HEAD_DIM, M_HEAD_DIM), M_HEAD_DIM ** -0.5),
        'w_k': nrm(ks[11], (L, M_HEADS, M_HEAD_DIM, M_HEAD_DIM), M_HEAD_DIM ** -0.5),
        'b_gate': jnp.concatenate([ig, fg], axis=2).reshape(L, N_GATES),
        'mnorm_w': 1.0 + nrm(ks[14], (L, M_W), 0.02),
        'm_skip': 1.0 + nrm(ks[15], (L, M_W), 0.02),
        'w_out': nrm(ks[16], (L, MIX_W, D_MODEL), MIX_W ** -0.5),
        'norm2_w': 1.0 + nrm(ks[17], (L, D_MODEL), 0.02),
        'w_up': nrm(ks[18], (L, D_MODEL, 2 * D_FF), D_MODEL ** -0.5),
        'b_up': nrm(ks[19], (L, 2 * D_FF), 0.02),
        'fconv_w': nrm(ks[20], (L, CONV_W, CONV_W, 2 * D_FF), 1.0 / CONV_W),
        'fconv_b': nrm(ks[21], (L, 2 * D_FF), 0.02),
        'w_down': nrm(ks[22], (L, D_FF, D_MODEL), D_FF ** -0.5),
        'b_down': nrm(ks[23], (L, D_MODEL), 0.02),
        'final_norm_w': 1.0 + nrm(ks[24], (D_MODEL,), 0.02),
    }


def reference(x, c, ctx, c_ctx, w_ada, b_ada, norm1_w, w_in, mconv_w, mconv_b, w_q, w_k, b_gate,
              mnorm_w, m_skip, w_out, norm2_w, w_up, b_up, fconv_w, fconv_b, w_down, b_down,
              final_norm_w):
    bsz = x.shape[0]
    for l in range(DEPTH):
        sh1, sc1, g1, sh2, sc2, g2 = _adaln(c, w_ada[l], b_ada[l])
        sh1c, sc1c, g1c, sh2c, sc2c, g2c = _adaln(c_ctx, w_ada[l], b_ada[l])
        mix_p = (w_in[l], w_out[l], mconv_w[l], mconv_b[l], w_q[l], w_k[l], b_gate[l], mnorm_w[l], m_skip[l])
        ffn_p = (w_up[l], b_up[l], fconv_w[l], fconv_b[l], w_down[l], b_down[l])

        hc = _modnorm(ctx, norm1_w[l], sh1c, sc1c)
        if l + 1 < DEPTH:
            zero = _zero_state(bsz)
            yc, st_f, st_b = _token_mixer(hc, *mix_p, zero, zero)
            ctx = ctx + g1c * yc
            ctx = ctx + g2c * _conv_ffn(_modnorm(ctx, norm2_w[l], sh2c, sc2c), *ffn_p, False)
        else:
            st_f, st_b = _context_states(hc @ w_in[l][:, F_W:MLSTM_END], mconv_w[l], mconv_b[l], w_k[l], b_gate[l])

        hx = _modnorm(x, norm1_w[l], sh1, sc1)
        y, _, _ = _token_mixer(hx, *mix_p, st_f, st_b)
        x = x + g1 * y
        x = x + g2 * _conv_ffn(_modnorm(x, norm2_w[l], sh2, sc2), *ffn_p, True)
    return _rmsnorm(x, final_norm_w)
```

```python
import functools
import math

import numpy as np
import jax
import jax.numpy as jnp
from jax import lax
from jax.experimental import pallas as pl
from jax.experimental.pallas import tpu as pltpu

F32 = jnp.float32
BF16 = jnp.bfloat16

D_MODEL = 1024
GRID_W = 64
F_W = 512
F_GROUPS = 4
F_GROUP_W = F_W // F_GROUPS
M_W = 512
M_HEADS = 4
M_HEAD_DIM = M_W // M_HEADS
N_GATES = 16
MLSTM_END = F_W + 2 * M_W + N_GATES
CHUNK = 128
D_FF = 5 * D_MODEL // 2
NORM_EPS = 1e-6

DFT_RADIX = 8
VMEM_LIMIT = 56 * 1024 * 1024


def _params(sem, vmem=VMEM_LIMIT):
    return pltpu.CompilerParams(dimension_semantics=sem, vmem_limit_bytes=vmem)


def _split3(a):
    hi = a.astype(BF16)
    r1 = a - hi.astype(F32)
    mid = r1.astype(BF16)
    lo = (r1 - mid.astype(F32)).astype(BF16)
    return hi, mid, lo


def _dot(a, b):
    return jnp.dot(a, b, preferred_element_type=F32)


def _dot_f32(a, b):
    a0, a1, a2 = _split3(a)
    b0, b1, b2 = _split3(b)
    return (_dot(a0, b0) + (_dot(a0, b1) + _dot(a1, b0))
            + (_dot(a1, b1) + _dot(a0, b2) + _dot(a2, b0)))


def _silu(x):
    return x * jax.nn.sigmoid(x)


def _rms(x, w):
    return x * lax.rsqrt(jnp.mean(x * x, axis=-1, keepdims=True) + NORM_EPS) * w


def _adaln_kernel(cond_ref, w_ref, b_ref, o_ref):
    s = _silu(cond_ref[...])
    o_ref[...] = _dot_f32(s, w_ref[...]) + b_ref[...]


def _adaln(cond, w_ada, b_ada):
    rows, d = cond.shape
    n = w_ada.shape[1]
    bn = 1536
    return pl.pallas_call(
        _adaln_kernel,
        grid=(n // bn,),
        in_specs=[pl.BlockSpec((rows, d), lambda j: (0, 0)),
                  pl.BlockSpec((d, bn), lambda j: (0, j)),
                  pl.BlockSpec((1, bn), lambda j: (0, j))],
        out_specs=pl.BlockSpec((rows, bn), lambda j: (0, j)),
        out_shape=jax.ShapeDtypeStruct((rows, n), F32),
        compiler_params=_params(("parallel",)),
        name="adaln",
    )(cond, w_ada, b_ada.reshape(1, n))


def _inproj_kernel(x_ref, mod_ref, nw_ref, wm_ref, wg_ref, wgt_ref, bg_ref, bgt_ref, *rest, tm, fourier):
    if fourier:
        wf_ref, xm_ref, v_ref, z_ref, g_ref, gt_ref, u_ref, h_scr = rest
    else:
        xm_ref, v_ref, g_ref, gt_ref = rest
    mod = mod_ref[0]
    shift, scale = mod[0:1, :], mod[1:2, :]
    h = _rms(x_ref[...], nw_ref[...]) * (1.0 + scale) + shift
    hb = h.astype(BF16)
    p = _dot(hb, wm_ref[...])
    xm_ref[...] = p[:, 0:M_W].astype(BF16)
    v_ref[...] = p[:, M_W:2 * M_W].astype(BF16)
    g_ref[...] = _dot(hb, wg_ref[...])[:, 0:N_GATES] + bg_ref[...]
    gt_ref[...] = lax.dot_general(wgt_ref[...], hb, (((1,), (1,)), ((), ())),
                                  preferred_element_type=F32) + bgt_ref[...]
    if fourier:
        z_ref[...] = p[:, 2 * M_W:3 * M_W].astype(BF16)
        nlb = h.shape[1] // 128
        for lb in range(nlb):
            h_scr[lb] = h[:, lb * 128:(lb + 1) * 128]
        sub = tm // DFT_RADIX
        hs = jnp.concatenate(
            [jnp.concatenate([h_scr[lb, pl.ds(n1, sub, stride=DFT_RADIX), :] for lb in range(nlb)], axis=1)
             for n1 in range(DFT_RADIX)], axis=0)
        uf = _dot(hs.astype(BF16), wf_ref[...])
        for n1 in range(DFT_RADIX):
            u_ref[0, n1] = uf[n1 * sub:(n1 + 1) * sub, :].astype(BF16)


def _inproj(x2d, mod, mod_row, norm_w, w_in, b_gate, seq_len, tm, fourier):
    n, d = x2d.shape
    tiles_per_seq = seq_len // tm
    nb = n // seq_len
    w_in_b = w_in.astype(BF16)
    w_xm_v = w_in_b[:, F_W:F_W + 2 * M_W]
    w_g = jnp.pad(w_in_b[:, F_W + 2 * M_W:MLSTM_END], ((0, 0), (0, 128 - N_GATES)))
    w_gt = w_in_b[:, F_W + 2 * M_W:MLSTM_END].T
    if fourier:
        w_main = jnp.concatenate([w_xm_v, w_in_b[:, MLSTM_END:]], axis=1)
    else:
        w_main = w_xm_v
    nm = w_main.shape[1]
    tok = lambda i: (i, 0)
    const = lambda i: (0, 0)
    in_specs = [pl.BlockSpec((tm, d), tok),
                pl.BlockSpec((1, 6, d), lambda i: (mod_row(i), 0, 0)),
                pl.BlockSpec((1, d), const),
                pl.BlockSpec((d, nm), const),
                pl.BlockSpec((d, 128), const),
                pl.BlockSpec((N_GATES, d), const),
                pl.BlockSpec((1, N_GATES), const),
                pl.BlockSpec((N_GATES, 1), const)]
    args = [x2d, mod, norm_w.reshape(1, d), w_main, w_g, w_gt,
            b_gate.reshape(1, N_GATES), b_gate.reshape(N_GATES, 1)]
    tokb = pl.BlockSpec((tm, M_W), tok)
    out_specs = [tokb, tokb]
    out_shape = [jax.ShapeDtypeStruct((n, M_W), BF16)] * 2
    if fourier:
        in_specs.append(pl.BlockSpec((d, F_W), const))
        args.append(w_in_b[:, :F_W])
        out_specs.append(tokb)
        out_shape.append(jax.ShapeDtypeStruct((n, M_W), BF16))
    out_specs += [pl.BlockSpec((tm, N_GATES), tok), pl.BlockSpec((N_GATES, tm), lambda i: (0, i))]
    out_shape += [jax.ShapeDtypeStruct((n, N_GATES), F32), jax.ShapeDtypeStruct((N_GATES, n), F32)]
    scratch = []
    if fourier:
        sub = tm // DFT_RADIX
        out_specs.append(pl.BlockSpec((1, DFT_RADIX, sub, F_W),
                                      lambda i: (i // tiles_per_seq, 0, i % tiles_per_seq, 0)))
        out_shape.append(jax.ShapeDtypeStruct((nb, DFT_RADIX, seq_len // DFT_RADIX, F_W), BF16))
        scratch.append(pltpu.VMEM((d // 128, tm, 128), F32))
    return pl.pallas_call(
        functools.partial(_inproj_kernel, tm=tm, fourier=fourier),
        grid=(n // tm,),
        in_specs=in_specs, out_specs=out_specs, out_shape=out_shape,
        scratch_shapes=scratch,
        compiler_params=_params(("parallel",)),
        name="inproj_latent" if fourier else "inproj_ctx",
    )(*args)


def _dft_tables(seq_len):
    sub = seq_len // DFT_RADIX
    k2 = np.arange(sub, dtype=np.int64)[None, :, None]
    n1 = np.arange(DFT_RADIX, dtype=np.int64)[:, None, None]
    n2 = np.arange(sub, dtype=np.int64)[None, None, :]
    ang = 2.0 * np.pi * ((k2 * (n1 + DFT_RADIX * n2)) % seq_len).astype(np.float64) / seq_len
    scale = 1.0 / math.sqrt(seq_len)
    return (jnp.asarray(np.cos(ang) * scale, dtype=F32).astype(BF16),
            jnp.asarray(np.sin(ang) * scale, dtype=F32).astype(BF16))


def _cadd(a, b):
    return (a[0] + b[0], a[1] + b[1])


def _csub(a, b):
    return (a[0] - b[0], a[1] - b[1])


def _cmul_w8(z, k):
    re, im = z
    r = math.sqrt(0.5)
    if k == 0:
        return z
    if k == 1:
        return ((re + im) * r, (im - re) * r)
    if k == 2:
        return (im, -re)
    return ((im - re) * r, (-re - im) * r)


def _dft8(z):
    def dft4(a, b, c, d):
        s0, s1 = _cadd(a, c), _csub(a, c)
        t0, t1 = _cadd(b, d), _csub(b, d)
        t1r = _cmul_w8(t1, 2)
        return [_cadd(s0, t0), _cadd(s1, t1r), _csub(s0, t0), _csub(s1, t1r)]
    ev = dft4(z[0], z[2], z[4], z[6])
    od = dft4(z[1], z[3], z[5], z[7])
    out = [None] * 8
    for k in range(4):
        t = _cmul_w8(od[k], k)
        out[k] = _cadd(ev[k], t)
        out[k + 4] = _csub(ev[k], t)
    return out


def _fourier_kernel(u_ref, tc_ref, ts_ref, p_ref, q_ref, ga_ref, gb_ref, *, sub, rb):
    for n1 in range(DFT_RADIX):
        un = u_ref[0, n1]
        ga_ref[n1] = _dot(tc_ref[n1], un)
        gb_ref[n1] = _dot(ts_ref[n1], un)

    def body(i, carry):
        r0 = pl.multiple_of(i * rb, rb)
        z = [(ga_ref[n1, pl.ds(r0, rb), :], -gb_ref[n1, pl.ds(r0, rb), :]) for n1 in range(DFT_RADIX)]
        x = _dft8(z)
        for k1 in range(DFT_RADIX):
            p_ref[pl.ds(k1 * sub + r0, rb), :] = x[k1][0].astype(BF16)
            q_ref[pl.ds(k1 * sub + r0, rb), :] = (-x[k1][1]).astype(BF16)
        return carry

    lax.fori_loop(0, sub // rb, body, 0)


def _fourier(u_perm, seq_len):
    nb = u_perm.shape[0]
    sub = seq_len // DFT_RADIX
    cb = 256
    tc, ts = _dft_tables(seq_len)
    return pl.pallas_call(
        functools.partial(_fourier_kernel, sub=sub, rb=32),
        grid=(nb, F_W // cb),
        in_specs=[pl.BlockSpec((1, DFT_RADIX, sub, cb), lambda b, j: (b, 0, 0, j)),
                  pl.BlockSpec((DFT_RADIX, sub, sub), lambda b, j: (0, 0, 0)),
                  pl.BlockSpec((DFT_RADIX, sub, sub), lambda b, j: (0, 0, 0))],
        out_specs=[pl.BlockSpec((seq_len, cb), lambda b, j: (b, j))] * 2,
        out_shape=[jax.ShapeDtypeStruct((nb * seq_len, F_W), BF16)] * 2,
        scratch_shapes=[pltpu.VMEM((DFT_RADIX, sub, cb), F32)] * 2,
        compiler_params=_params(("parallel", "parallel")),
        name="fourier",
    )(u_perm, tc, ts)


def _prep_kernel(xm_ref, prev_ref, next_ref, cw_ref, cb_ref, wq_ref, wkt_ref, a_ref, q_ref, kt_ref,
                 *, t, tiles_per_seq):
    i = pl.program_id(0)
    first = (i % tiles_per_seq) == 0
    last = (i % tiles_per_seq) == tiles_per_seq - 1
    x = xm_ref[...].astype(F32)
    prev_row = jnp.where(first, 0.0, prev_ref[15:16, :].astype(F32))
    next_row = jnp.where(last, 0.0, next_ref[0:1, :].astype(F32))
    rows = lax.broadcasted_iota(jnp.int32, (t, 1), 0)
    xp = jnp.where(rows == 0, prev_row, pltpu.roll(x, 1, 0))
    xn = jnp.where(rows == t - 1, next_row, pltpu.roll(x, t - 1, 0))
    cw = cw_ref[...]
    a = _silu(cw[0:1, :] * xp + cw[1:2, :] * x + cw[2:3, :] * xn + cb_ref[...])
    ab = a.astype(BF16)
    a_ref[...] = ab
    for pair in range(2):
        sl = slice(pair * 256, (pair + 1) * 256)
        q_ref[:, sl] = _dot(ab[:, sl], wq_ref[pair]).astype(BF16)
        kt = lax.dot_general(wkt_ref[pair], ab[:, sl], (((1,), (1,)), ((), ())),
                             preferred_element_type=F32)
        kt_ref[0, sl, :] = (kt * M_HEAD_DIM ** -0.5).astype(BF16)


def _pair_blockdiag(w):
    z = jnp.zeros_like(w[0])
    return jnp.stack([jnp.block([[w[0], z], [z, w[1]]]), jnp.block([[w[2], z], [z, w[3]]])])


def _prep(xm, mconv_w, mconv_b, w_q, w_k, seq_len, t):
    n = xm.shape[0]
    nb = n // seq_len
    tiles_per_seq = seq_len // t
    hb = t // 16
    nhalo = n // 16
    wq_bd = _pair_blockdiag(w_q.astype(BF16))
    wkt_bd = _pair_blockdiag(jnp.swapaxes(w_k, 1, 2).astype(BF16))
    return pl.pallas_call(
        functools.partial(_prep_kernel, t=t, tiles_per_seq=tiles_per_seq),
        grid=(n // t,),
        in_specs=[pl.BlockSpec((t, M_W), lambda i: (i, 0)),
                  pl.BlockSpec((16, M_W), lambda i: (jnp.maximum(i * hb - 1, 0), 0)),
                  pl.BlockSpec((16, M_W), lambda i: (jnp.minimum((i + 1) * hb, nhalo - 1), 0)),
                  pl.BlockSpec((3, M_W), lambda i: (0, 0)),
                  pl.BlockSpec((1, M_W), lambda i: (0, 0)),
                  pl.BlockSpec((2, 256, 256), lambda i: (0, 0, 0)),
                  pl.BlockSpec((2, 256, 256), lambda i: (0, 0, 0))],
        out_specs=[pl.BlockSpec((t, M_W), lambda i: (i, 0)),
                   pl.BlockSpec((t, M_W), lambda i: (i, 0)),
                   pl.BlockSpec((1, M_W, t), lambda i: (i // tiles_per_seq, 0, i % tiles_per_seq))],
        out_shape=[jax.ShapeDtypeStruct((n, M_W), BF16),
                   jax.ShapeDtypeStruct((n, M_W), BF16),
                   jax.ShapeDtypeStruct((nb, M_W, seq_len), BF16)],
        compiler_params=_params(("parallel",)),
        name="prep_%d" % seq_len,
    )(xm, xm, xm, mconv_w, mconv_b.reshape(1, M_W), wq_bd, wkt_bd)


def _log_sigmoid(x):
    return jnp.minimum(x, 0.0) - jnp.log1p(jnp.exp(-jnp.abs(x)))


def _tri(reverse):
    r = lax.broadcasted_iota(jnp.int32, (CHUNK, CHUNK), 0)
    c = lax.broadcasted_iota(jnp.int32, (CHUNK, CHUNK), 1)
    return (c >= r) if reverse else (c <= r)


def _cumsum_gates(g, gt, reverse):
    causal = _tri(reverse)
    t_col = causal.astype(BF16)
    t_row = _tri(not reverse).astype(BF16)
    lf0, lf1, lf2 = _split3(_log_sigmoid(g))
    bcol = _dot(t_col, lf0) + _dot(t_col, lf1) + _dot(t_col, lf2)
    lt0, lt1, lt2 = _split3(_log_sigmoid(gt))
    brow = _dot(lt0, t_row) + _dot(lt1, t_row) + _dot(lt2, t_row)
    return causal, bcol, brow


def _state_step(kt, vaug, rowterm, b_end, c_aug, m):
    g = b_end + rowterm
    m_new = jnp.maximum(b_end + m, jnp.max(g, axis=-1, keepdims=True))
    w_old = jnp.exp(b_end + m - m_new)
    w_tok = jnp.exp(g - m_new)
    kw = (kt.astype(F32) * w_tok).astype(BF16)
    return w_old * c_aug + _dot(kw, vaug), m_new


def _chunk_step(q, kt, vaug, causal, bcol, brow, li_row, c_aug, m, reverse, with_output=True):
    rowterm = li_row - brow
    end = 0 if reverse else CHUNK - 1
    b_end = brow[:, end:end + 1]
    h = None
    if with_output:
        dmat = jnp.where(causal, bcol + rowterm, -jnp.inf)
        inter = bcol + m
        m_t = jnp.maximum(inter, jnp.max(dmat, axis=-1, keepdims=True))
        w_inter = jnp.exp(inter - m_t)
        s = _dot(q, kt) * jnp.exp(dmat - m_t)
        nd = w_inter * _dot(q, c_aug.astype(BF16)) + _dot(s.astype(BF16), vaug)
        den = nd[:, M_HEAD_DIM:M_HEAD_DIM + 1]
        h = nd[:, :M_HEAD_DIM] / jnp.maximum(jnp.abs(den), jnp.exp(-m_t))
    c_new, m_new = _state_step(kt, vaug, rowterm, b_end, c_aug, m)
    return h, c_new, m_new


def _scan_kernel(qf_ref, ktf_ref, vf_ref, gf_ref, gtf_ref,
                 qb_ref, ktb_ref, vb_ref, gb_ref, gtb_ref,
                 ktc_ref, vc_ref, gc_ref, gtc_ref,
                 hf_ref, hb_ref, c_scr, m_scr, *, ctx_chunks):
    c = pl.program_id(1)
    lane = lax.broadcasted_iota(jnp.int32, (CHUNK, M_HEAD_DIM), 1)
    ones_col = jnp.where(lane == 0, 1.0, 0.0).astype(BF16)

    def run_chunk(q, kt, v, g, gt, d, with_output):
        reverse = d == 1
        causal, bcol, brow = _cumsum_gates(g, gt, reverse)
        outs = []
        for hd in range(M_HEADS):
            sl = slice(hd * M_HEAD_DIM, (hd + 1) * M_HEAD_DIM)
            gi, gf = d * 8 + hd, d * 8 + 4 + hd
            vaug = jnp.concatenate([v[:, sl], ones_col], axis=-1)
            r = d * M_HEADS + hd
            h, c_new, m_new = _chunk_step(
                q[:, sl] if with_output else None, kt[sl, :], vaug, causal,
                bcol[:, gf:gf + 1], brow[gf:gf + 1, :], gt[gi:gi + 1, :],
                c_scr[r], m_scr[r:r + 1, 0:1], reverse, with_output)
            c_scr[r] = c_new
            m_scr[r:r + 1, :] = jnp.broadcast_to(m_new, (1, 128))
            outs.append(h)
        return outs

    @pl.when(c == 0)
    def _():
        c_scr[...] = jnp.zeros_like(c_scr)
        m_scr[...] = jnp.zeros_like(m_scr)
        for d in range(2):
            order = range(ctx_chunks) if d == 0 else range(ctx_chunks - 1, -1, -1)
            for j in order:
                rs = slice(j * CHUNK, (j + 1) * CHUNK)
                run_chunk(None, ktc_ref[0, :, rs], vc_ref[rs, :], gc_ref[rs, :], gtc_ref[:, rs], d, False)

    hf = run_chunk(qf_ref[...], ktf_ref[0], vf_ref[...], gf_ref[...], gtf_ref[...], 0, True)
    hb = run_chunk(qb_ref[...], ktb_ref[0], vb_ref[...], gb_ref[...], gtb_ref[...], 1, True)
    for hd in range(M_HEADS):
        sl = slice(hd * M_HEAD_DIM, (hd + 1) * M_HEAD_DIM)
        hf_ref[:, sl] = hf[hd].astype(BF16)
        hb_ref[:, sl] = hb[hd].astype(BF16)


def _scan(q, kt, v, g, gt, kt_c, v_c, g_c, gt_c, seq_len, ctx_len):
    n = q.shape[0]
    nb = n // seq_len
    nc = seq_len // CHUNK
    fwd = lambda b, c: (b * nc + c, 0)
    bwd = lambda b, c: (b * nc + nc - 1 - c, 0)
    fwd_t = lambda b, c: (0, b * nc + c)
    bwd_t = lambda b, c: (0, b * nc + nc - 1 - c)
    tokb = lambda im: pl.BlockSpec((CHUNK, M_W), im)

    def side(tok_map, tok_map_t, kt_map):
        return [tokb(tok_map), pl.BlockSpec((1, M_W, CHUNK), kt_map), tokb(tok_map),
                pl.BlockSpec((CHUNK, N_GATES), tok_map), pl.BlockSpec((N_GATES, CHUNK), tok_map_t)]

    in_specs = (side(fwd, fwd_t, lambda b, c: (b, 0, c))
                + side(bwd, bwd_t, lambda b, c: (b, 0, nc - 1 - c))
                + [pl.BlockSpec((1, M_W, ctx_len), lambda b, c: (b, 0, 0)),
                   pl.BlockSpec((ctx_len, M_W), lambda b, c: (b, 0)),
                   pl.BlockSpec((ctx_len, N_GATES), lambda b, c: (b, 0)),
                   pl.BlockSpec((N_GATES, ctx_len), lambda b, c: (0, b))])
    return pl.pallas_call(
        functools.partial(_scan_kernel, ctx_chunks=ctx_len // CHUNK),
        grid=(nb, nc),
        in_specs=in_specs,
        out_specs=[tokb(fwd), tokb(bwd)],
        out_shape=[jax.ShapeDtypeStruct((n, M_W), BF16)] * 2,
        scratch_shapes=[pltpu.VMEM((2 * M_HEADS, M_HEAD_DIM, 2 * M_HEAD_DIM), F32),
                        pltpu.VMEM((2 * M_HEADS, 128), F32)],
        compiler_params=_params(("parallel", "arbitrary")),
        name="mlstm_scan",
    )(q, kt, v, g, gt, q, kt, v, g, gt, kt_c, v_c, g_c, gt_c)


def _mixout_kernel(x_ref, mod_ref, p_ref, q_ref, hf_ref, hb_ref, a_ref, z_ref, mnw_ref, msk_ref,
                   wc_ref, wo_ref, n2w_ref, x1_ref, h2_ref):
    mod = mod_ref[0]
    gate1, shift2, scale2 = mod[2:3, :], mod[3:4, :], mod[4:5, :]
    yf = _dot(p_ref[...], wc_ref[0]) + _dot(q_ref[...], wc_ref[1])
    h = hf_ref[...].astype(F32) + hb_ref[...].astype(F32)
    parts = []
    for hd in range(M_HEADS):
        hh = h[:, hd * M_HEAD_DIM:(hd + 1) * M_HEAD_DIM]
        mu = jnp.mean(hh, axis=-1, keepdims=True)
        dlt = hh - mu
        var = jnp.mean(dlt * dlt, axis=-1, keepdims=True)
        parts.append(dlt * lax.rsqrt(var + NORM_EPS))
    hn = jnp.concatenate(parts, axis=-1) * mnw_ref[...]
    ym = (hn + msk_ref[...] * a_ref[...].astype(F32)) * _silu(z_ref[...].astype(F32))
    y = _dot(yf.astype(BF16), wo_ref[0]) + _dot(ym.astype(BF16), wo_ref[1])
    x1 = x_ref[...] + gate1 * y
    x1_ref[...] = x1
    h2_ref[...] = (_rms(x1, n2w_ref[...]) * (1.0 + scale2) + shift2).astype(BF16)


def _channel_dft():
    k = np.arange(F_GROUP_W, dtype=np.int64)
    ang = 2.0 * np.pi * ((k[:, None] * k[None, :]) % F_GROUP_W).astype(np.float64) / F_GROUP_W
    scale = 1.0 / math.sqrt(F_GROUP_W)
    eye = np.eye(F_GROUPS)
    wc = np.stack([np.kron(eye, np.cos(ang) * scale), -np.kron(eye, np.sin(ang) * scale)])
    return jnp.asarray(wc, dtype=F32).astype(BF16)


def _mixout(x2d, mod, p, q, hf, hb, a, z, mnorm_w, m_skip, w_out, norm2_w, seq_len, tm):
    n, d = x2d.shape
    tiles_per_seq = seq_len // tm
    tok = lambda i: (i, 0)
    half = pl.BlockSpec((tm, F_W), tok)
    full = pl.BlockSpec((tm, d), tok)
    vec = lambda w: pl.BlockSpec((1, w), lambda i: (0, 0))
    return pl.pallas_call(
        _mixout_kernel,
        grid=(n // tm,),
        in_specs=[full, pl.BlockSpec((1, 6, d), lambda i: (i // tiles_per_seq, 0, 0)),
                  half, half, half, half, half, half, vec(M_W), vec(M_W),
                  pl.BlockSpec((2, F_W, F_W), lambda i: (0, 0, 0)),
                  pl.BlockSpec((2, F_W, d), lambda i: (0, 0, 0)),
                  vec(d)],
        out_specs=[full, full],
        out_shape=[jax.ShapeDtypeStruct((n, d), F32), jax.ShapeDtypeStruct((n, d), BF16)],
        compiler_params=_params(("parallel",)),
        name="mixout",
    )(x2d, mod, p, q, hf, hb, a, z, mnorm_w.reshape(1, M_W), m_skip.reshape(1, M_W),
      _channel_dft(), w_out.astype(BF16).reshape(2, F_W, d), norm2_w.reshape(1, d))


def _ffn_kernel(h_ref, hp_ref, hn_ref, x1_ref, mod_ref, wu_ref, bu_ref, cw_ref, cb_ref, wd_ref, bd_ref,
                fw_ref, o_ref, acc_ref, *, rows, tiles, cw):
    t = pl.program_id(1)
    nt = rows * GRID_W
    ne = nt + 2 * GRID_W
    hcat = jnp.concatenate([hp_ref[...], h_ref[...], hn_ref[...]], axis=0)
    ridx = lax.broadcasted_iota(jnp.int32, (ne, 1), 0)
    inside = jnp.logical_and(jnp.logical_or(ridx >= GRID_W, t > 0),
                             jnp.logical_or(ridx < nt + GRID_W, t < tiles - 1))
    col = ridx & (GRID_W - 1)
    acc_ref[...] = jnp.zeros_like(acc_ref)

    def body(j, carry):
        u = _dot(hcat, wu_ref[j]) + bu_ref[j]
        u = jnp.where(inside, u, 0.0)
        um = jnp.where(col == 0, 0.0, pltpu.roll(u, 1, 0))
        up = jnp.where(col == GRID_W - 1, 0.0, pltpu.roll(u, ne - 1, 0))
        w = cw_ref[j]
        conv = cb_ref[j]
        for dr in range(3):
            rs = slice(dr * GRID_W, dr * GRID_W + nt)
            conv = conv + (w[3 * dr:3 * dr + 1, :] * um[rs, :] + w[3 * dr + 1:3 * dr + 2, :] * u[rs, :]
                           + w[3 * dr + 2:3 * dr + 3, :] * up[rs, :])
        act = conv[:, :cw] * _silu(conv[:, cw:])
        acc_ref[...] += _dot(act.astype(BF16), wd_ref[j])
        return carry

    lax.fori_loop(0, D_FF // cw, body, 0)
    mod = mod_ref[0]
    y = acc_ref[...] + bd_ref[...]
    o_ref[...] = _rms(x1_ref[...] + mod[5:6, :] * y, fw_ref[...])


def _ffn(h2, x1, mod, w_up, b_up, fconv_w, fconv_b, w_down, b_down, final_norm_w, seq_len, rows, cw):
    n, d = x1.shape
    nb = n // seq_len
    grid_h = seq_len // GRID_W
    tiles = grid_h // rows
    nt = rows * GRID_W
    nj = D_FF // cw

    def pair(w):
        lead = w.shape[:-1]
        w2 = w.reshape(lead + (2, nj, cw))
        w2 = jnp.moveaxis(w2, -2, 0)
        return w2.reshape((nj,) + lead + (2 * cw,))

    wu = pair(w_up.astype(BF16))
    bu = pair(b_up.reshape(1, 2 * D_FF))
    cwt = pair(fconv_w.reshape(9, 2 * D_FF))
    cbt = pair(fconv_b.reshape(1, 2 * D_FF))
    wd = w_down.astype(BF16).reshape(nj, cw, d)
    nblk = n // GRID_W
    const3 = lambda b, t: (0, 0, 0)
    return pl.pallas_call(
        functools.partial(_ffn_kernel, rows=rows, tiles=tiles, cw=cw),
        grid=(nb, tiles),
        in_specs=[pl.BlockSpec((nt, d), lambda b, t: (b * tiles + t, 0)),
                  pl.BlockSpec((GRID_W, d), lambda b, t: (jnp.maximum((b * tiles + t) * rows - 1, 0), 0)),
                  pl.BlockSpec((GRID_W, d), lambda b, t: (jnp.minimum((b * tiles + t + 1) * rows, nblk - 1), 0)),
                  pl.BlockSpec((nt, d), lambda b, t: (b * tiles + t, 0)),
                  pl.BlockSpec((1, 6, d), lambda b, t: (b, 0, 0)),
                  pl.BlockSpec((nj, d, 2 * cw), const3),
                  pl.BlockSpec((nj, 1, 2 * cw), const3),
                  pl.BlockSpec((nj, 9, 2 * cw), const3),
                  pl.BlockSpec((nj, 1, 2 * cw), const3),
                  pl.BlockSpec((nj, cw, d), const3),
                  pl.BlockSpec((1, d), lambda b, t: (0, 0)),
                  pl.BlockSpec((1, d), lambda b, t: (0, 0))],
        out_specs=pl.BlockSpec((nt, d), lambda b, t: (b * tiles + t, 0)),
        out_shape=jax.ShapeDtypeStruct((n, d), F32),
        scratch_shapes=[pltpu.VMEM((nt, d), F32)],
        compiler_params=_params(("parallel", "parallel")),
        name="conv_ffn",
    )(h2, h2, h2, x1, mod, wu, bu, cwt, cbt, wd, b_down.reshape(1, d), final_norm_w.reshape(1, d))


def kernel(x, c, ctx, c_ctx, w_ada, b_ada, norm1_w, w_in, mconv_w, mconv_b, w_q, w_k, b_gate, mnorm_w, m_skip,
           w_out, norm2_w, w_up, b_up, fconv_w, fconv_b, w_down, b_down, final_norm_w):
    bsz, seq_len, d = x.shape
    ctx_len = ctx.shape[1]
    assert w_ada.shape[0] == 1, "single-layer kernel"
    cond = jnp.concatenate([c, c_ctx[None, :], jnp.zeros((16 - bsz - 1, d), F32)], axis=0)
    mod = _adaln(cond, w_ada[0], b_ada[0]).reshape(16, 6, d)

    x2d = x.reshape(bsz * seq_len, d)
    ctx2d = ctx.reshape(bsz * ctx_len, d)

    xm_c, v_c, g_c, gt_c = _inproj(ctx2d, mod, lambda i: bsz, norm1_w[0], w_in[0], b_gate[0],
                                   ctx_len, ctx_len, False)
    _, _, kt_c = _prep(xm_c, mconv_w[0], mconv_b[0], w_q[0], w_k[0], ctx_len, ctx_len)

    tm = 512
    xm, v, z, g, gt, u_perm = _inproj(x2d, mod, lambda i: i // (seq_len // tm), norm1_w[0], w_in[0],
                                      b_gate[0], seq_len, tm, True)
    p, q_im = _fourier(u_perm, seq_len)
    a, q, kt = _prep(xm, mconv_w[0], mconv_b[0], w_q[0], w_k[0], seq_len, 512)
    hf, hb = _scan(q, kt, v, g, gt, kt_c, v_c, g_c, gt_c, seq_len, ctx_len)
    x1, h2 = _mixout(x2d, mod, p, q_im, hf, hb, a, z, mnorm_w[0], m_skip[0], w_out[0], norm2_w[0],
                     seq_len, tm)
    out = _ffn(h2, x1, mod, w_up[0], b_up[0], fconv_w[0], fconv_b[0], w_down[0], b_down[0],
               final_norm_w, seq_len, 8, 256)
    return out.reshape(bsz, seq_len, d)
```

```python
import functools
import math

import numpy as np
import jax
import jax.numpy as jnp
from jax import lax
from jax.experimental import pallas as pl
from jax.experimental.pallas import tpu as pltpu

F32 = jnp.float32
BF16 = jnp.bfloat16

D_MODEL = 1024
GRID_W = 64
F_W = 512
F_GROUPS = 4
F_GROUP_W = F_W // F_GROUPS
M_W = 512
M_HEADS = 4
M_HEAD_DIM = M_W // M_HEADS
N_GATES = 16
MLSTM_END = F_W + 2 * M_W + N_GATES
CHUNK = 128
D_FF = 5 * D_MODEL // 2
NORM_EPS = 1e-6

DFT_RADIX = 8
VMEM_LIMIT = 56 * 1024 * 1024
FFN_FLAGS = None


def _params(sem, vmem=VMEM_LIMIT, flags=None):
    return pltpu.CompilerParams(dimension_semantics=sem, vmem_limit_bytes=vmem, flags=flags)


def _split3(a):
    hi = a.astype(BF16)
    r1 = a - hi.astype(F32)
    mid = r1.astype(BF16)
    lo = (r1 - mid.astype(F32)).astype(BF16)
    return hi, mid, lo


def _dot(a, b):
    return jnp.dot(a, b, preferred_element_type=F32)


def _dot_f32(a, b):
    a0, a1, a2 = _split3(a)
    b0, b1, b2 = _split3(b)
    return (_dot(a0, b0) + (_dot(a0, b1) + _dot(a1, b0))
            + (_dot(a1, b1) + _dot(a0, b2) + _dot(a2, b0)))


def _silu(x):
    return x * jax.nn.sigmoid(x)


def _rms(x, w):
    return x * lax.rsqrt(jnp.mean(x * x, axis=-1, keepdims=True) + NORM_EPS) * w


def _adaln_kernel(cond_ref, w_ref, b_ref, o_ref):
    s = _silu(cond_ref[...])
    o_ref[...] = _dot_f32(s, w_ref[...]) + b_ref[...]


def _adaln(cond, w_ada, b_ada):
    rows, d = cond.shape
    n = w_ada.shape[1]
    bn = 1536
    return pl.pallas_call(
        _adaln_kernel,
        grid=(n // bn,),
        in_specs=[pl.BlockSpec((rows, d), lambda j: (0, 0)),
                  pl.BlockSpec((d, bn), lambda j: (0, j)),
                  pl.BlockSpec((1, bn), lambda j: (0, j))],
        out_specs=pl.BlockSpec((rows, bn), lambda j: (0, j)),
        out_shape=jax.ShapeDtypeStruct((rows, n), F32),
        compiler_params=_params(("parallel",)),
        name="adaln",
    )(cond, w_ada, b_ada.reshape(1, n))


def _inproj_kernel(x_ref, mod_ref, nw_ref, wm_ref, wgt_ref, bgt_ref, *rest, tm, fourier):
    if fourier:
        wf_ref, xm_ref, v_ref, z_ref, gt_ref, u_ref, h_scr = rest
    else:
        xm_ref, v_ref, gt_ref = rest
    mod = mod_ref[0]
    shift, scale = mod[0:1, :], mod[1:2, :]
    h = _rms(x_ref[...], nw_ref[...]) * (1.0 + scale) + shift
    hb = h.astype(BF16)
    p = _dot(hb, wm_ref[...])
    xm_ref[...] = p[:, 0:M_W].astype(BF16)
    v_ref[...] = p[:, M_W:2 * M_W].astype(BF16)
    gt_ref[...] = lax.dot_general(wgt_ref[...], hb, (((1,), (1,)), ((), ())),
                                  preferred_element_type=F32) + bgt_ref[...]
    if fourier:
        z_ref[...] = p[:, 2 * M_W:3 * M_W].astype(BF16)
        nlb = h.shape[1] // 128
        for lb in range(nlb):
            h_scr[lb] = h[:, lb * 128:(lb + 1) * 128]
        sub = tm // DFT_RADIX
        hs = jnp.concatenate(
            [jnp.concatenate([h_scr[lb, pl.ds(n1, sub, stride=DFT_RADIX), :] for lb in range(nlb)], axis=1)
             for n1 in range(DFT_RADIX)], axis=0)
        uf = _dot(hs.astype(BF16), wf_ref[...])
        for n1 in range(DFT_RADIX):
            u_ref[0, n1] = uf[n1 * sub:(n1 + 1) * sub, :].astype(BF16)


def _inproj(x2d, mod, mod_row, norm_w, w_in, b_gate, seq_len, tm, fourier):
    n, d = x2d.shape
    tiles_per_seq = seq_len // tm
    nb = n // seq_len
    w_in_b = w_in.astype(BF16)
    w_xm_v = w_in_b[:, F_W:F_W + 2 * M_W]
    w_gt = w_in_b[:, F_W + 2 * M_W:MLSTM_END].T
    if fourier:
        w_main = jnp.concatenate([w_xm_v, w_in_b[:, MLSTM_END:]], axis=1)
    else:
        w_main = w_xm_v
    nm = w_main.shape[1]
    tok = lambda i: (i, 0)
    const = lambda i: (0, 0)
    in_specs = [pl.BlockSpec((tm, d), tok),
                pl.BlockSpec((1, 6, d), lambda i: (mod_row(i), 0, 0)),
                pl.BlockSpec((1, d), const),
                pl.BlockSpec((d, nm), const),
                pl.BlockSpec((N_GATES, d), const),
                pl.BlockSpec((N_GATES, 1), const)]
    args = [x2d, mod, norm_w.reshape(1, d), w_main, w_gt, b_gate.reshape(N_GATES, 1)]
    tokb = pl.BlockSpec((tm, M_W), tok)
    out_specs = [tokb, tokb]
    out_shape = [jax.ShapeDtypeStruct((n, M_W), BF16)] * 2
    if fourier:
        in_specs.append(pl.BlockSpec((d, F_W), const))
        args.append(w_in_b[:, :F_W])
        out_specs.append(tokb)
        out_shape.append(jax.ShapeDtypeStruct((n, M_W), BF16))
    out_specs.append(pl.BlockSpec((N_GATES, tm), lambda i: (0, i)))
    out_shape.append(jax.ShapeDtypeStruct((N_GATES, n), F32))
    scratch = []
    if fourier:
        sub = tm // DFT_RADIX
        out_specs.append(pl.BlockSpec((1, DFT_RADIX, sub, F_W),
                                      lambda i: (i // tiles_per_seq, 0, i % tiles_per_seq, 0)))
        out_shape.append(jax.ShapeDtypeStruct((nb, DFT_RADIX, seq_len // DFT_RADIX, F_W), BF16))
        scratch.append(pltpu.VMEM((d // 128, tm, 128), F32))
    return pl.pallas_call(
        functools.partial(_inproj_kernel, tm=tm, fourier=fourier),
        grid=(n // tm,),
        in_specs=in_specs, out_specs=out_specs, out_shape=out_shape,
        scratch_shapes=scratch,
        compiler_params=_params(("parallel",)),
        name="inproj_latent" if fourier else "inproj_ctx",
    )(*args)


def _dft_tables(seq_len):
    sub = seq_len // DFT_RADIX
    k2 = np.arange(sub, dtype=np.int64)[None, :, None]
    n1 = np.arange(DFT_RADIX, dtype=np.int64)[:, None, None]
    n2 = np.arange(sub, dtype=np.int64)[None, None, :]
    ang = 2.0 * np.pi * ((k2 * (n1 + DFT_RADIX * n2)) % seq_len).astype(np.float64) / seq_len
    scale = 1.0 / math.sqrt(seq_len)
    return (jnp.asarray(np.cos(ang) * scale, dtype=F32).astype(BF16),
            jnp.asarray(np.sin(ang) * scale, dtype=F32).astype(BF16))


def _cadd(a, b):
    return (a[0] + b[0], a[1] + b[1])


def _csub(a, b):
    return (a[0] - b[0], a[1] - b[1])


def _cmul_w8(z, k):
    re, im = z
    r = math.sqrt(0.5)
    if k == 0:
        return z
    if k == 1:
        return ((re + im) * r, (im - re) * r)
    if k == 2:
        return (im, -re)
    return ((im - re) * r, (-re - im) * r)


def _dft8(z):
    def dft4(a, b, c, d):
        s0, s1 = _cadd(a, c), _csub(a, c)
        t0, t1 = _cadd(b, d), _csub(b, d)
        t1r = _cmul_w8(t1, 2)
        return [_cadd(s0, t0), _cadd(s1, t1r), _csub(s0, t0), _csub(s1, t1r)]
    ev = dft4(z[0], z[2], z[4], z[6])
    od = dft4(z[1], z[3], z[5], z[7])
    out = [None] * 8
    for k in range(4):
        t = _cmul_w8(od[k], k)
        out[k] = _cadd(ev[k], t)
        out[k + 4] = _csub(ev[k], t)
    return out


def _fourier_kernel(u_ref, tc_ref, ts_ref, p_ref, q_ref, ga_ref, gb_ref, *, sub, rb):
    for n1 in range(DFT_RADIX):
        un = u_ref[0, n1]
        ga_ref[n1] = _dot(tc_ref[n1], un)
        gb_ref[n1] = _dot(ts_ref[n1], un)

    def body(i, carry):
        r0 = pl.multiple_of(i * rb, rb)
        z = [(ga_ref[n1, pl.ds(r0, rb), :], -gb_ref[n1, pl.ds(r0, rb), :]) for n1 in range(DFT_RADIX)]
        x = _dft8(z)
        for k1 in range(DFT_RADIX):
            p_ref[pl.ds(k1 * sub + r0, rb), :] = x[k1][0].astype(BF16)
            q_ref[pl.ds(k1 * sub + r0, rb), :] = (-x[k1][1]).astype(BF16)
        return carry

    lax.fori_loop(0, sub // rb, body, 0)


def _fourier(u_perm, seq_len):
    nb = u_perm.shape[0]
    sub = seq_len // DFT_RADIX
    cb = 256
    tc, ts = _dft_tables(seq_len)
    return pl.pallas_call(
        functools.partial(_fourier_kernel, sub=sub, rb=32),
        grid=(nb, F_W // cb),
        in_specs=[pl.BlockSpec((1, DFT_RADIX, sub, cb), lambda b, j: (b, 0, 0, j)),
                  pl.BlockSpec((DFT_RADIX, sub, sub), lambda b, j: (0, 0, 0)),
                  pl.BlockSpec((DFT_RADIX, sub, sub), lambda b, j: (0, 0, 0))],
        out_specs=[pl.BlockSpec((seq_len, cb), lambda b, j: (b, j))] * 2,
        out_shape=[jax.ShapeDtypeStruct((nb * seq_len, F_W), BF16)] * 2,
        scratch_shapes=[pltpu.VMEM((DFT_RADIX, sub, cb), F32)] * 2,
        compiler_params=_params(("parallel", "parallel")),
        name="fourier",
    )(u_perm, tc, ts)


def _prep_kernel(xm_ref, prev_ref, next_ref, cw_ref, cb_ref, wq_ref, wkt_ref, a_ref, q_ref, kt_ref,
                 *, t, tiles_per_seq):
    i = pl.program_id(0)
    first = (i % tiles_per_seq) == 0
    last = (i % tiles_per_seq) == tiles_per_seq - 1
    x = xm_ref[...].astype(F32)
    prev_row = jnp.where(first, 0.0, prev_ref[15:16, :].astype(F32))
    next_row = jnp.where(last, 0.0, next_ref[0:1, :].astype(F32))
    rows = lax.broadcasted_iota(jnp.int32, (t, 1), 0)
    xp = jnp.where(rows == 0, prev_row, pltpu.roll(x, 1, 0))
    xn = jnp.where(rows == t - 1, next_row, pltpu.roll(x, t - 1, 0))
    cw = cw_ref[...]
    a = _silu(cw[0:1, :] * xp + cw[1:2, :] * x + cw[2:3, :] * xn + cb_ref[...])
    ab = a.astype(BF16)
    a_ref[...] = ab
    for pair in range(2):
        sl = slice(pair * 256, (pair + 1) * 256)
        q_ref[:, sl] = _dot(ab[:, sl], wq_ref[pair]).astype(BF16)
        kt = lax.dot_general(wkt_ref[pair], ab[:, sl], (((1,), (1,)), ((), ())),
                             preferred_element_type=F32)
        kt_ref[0, sl, :] = (kt * M_HEAD_DIM ** -0.5).astype(BF16)


def _pair_blockdiag(w):
    z = jnp.zeros_like(w[0])
    return jnp.stack([jnp.block([[w[0], z], [z, w[1]]]), jnp.block([[w[2], z], [z, w[3]]])])


def _prep(xm, mconv_w, mconv_b, w_q, w_k, seq_len, t):
    n = xm.shape[0]
    nb = n // seq_len
    tiles_per_seq = seq_len // t
    hb = t // 16
    nhalo = n // 16
    wq_bd = _pair_blockdiag(w_q.astype(BF16))
    wkt_bd = _pair_blockdiag(jnp.swapaxes(w_k, 1, 2).astype(BF16))
    return pl.pallas_call(
        functools.partial(_prep_kernel, t=t, tiles_per_seq=tiles_per_seq),
        grid=(n // t,),
        in_specs=[pl.BlockSpec((t, M_W), lambda i: (i, 0)),
                  pl.BlockSpec((16, M_W), lambda i: (jnp.maximum(i * hb - 1, 0), 0)),
                  pl.BlockSpec((16, M_W), lambda i: (jnp.minimum((i + 1) * hb, nhalo - 1), 0)),
                  pl.BlockSpec((3, M_W), lambda i: (0, 0)),
                  pl.BlockSpec((1, M_W), lambda i: (0, 0)),
                  pl.BlockSpec((2, 256, 256), lambda i: (0, 0, 0)),
                  pl.BlockSpec((2, 256, 256), lambda i: (0, 0, 0))],
        out_specs=[pl.BlockSpec((t, M_W), lambda i: (i, 0)),
                   pl.BlockSpec((t, M_W), lambda i: (i, 0)),
                   pl.BlockSpec((1, M_W, t), lambda i: (i // tiles_per_seq, 0, i % tiles_per_seq))],
        out_shape=[jax.ShapeDtypeStruct((n, M_W), BF16),
                   jax.ShapeDtypeStruct((n, M_W), BF16),
                   jax.ShapeDtypeStruct((nb, M_W, seq_len), BF16)],
        compiler_params=_params(("parallel",)),
        name="prep_%d" % seq_len,
    )(xm, xm, xm, mconv_w, mconv_b.reshape(1, M_W), wq_bd, wkt_bd)


def _log_sigmoid(x):
    return jnp.minimum(x, 0.0) - jnp.log1p(jnp.exp(-jnp.abs(x)))


def _tri(reverse):
    r = lax.broadcasted_iota(jnp.int32, (CHUNK, CHUNK), 0)
    c = lax.broadcasted_iota(jnp.int32, (CHUNK, CHUNK), 1)
    return (c >= r) if reverse else (c <= r)


def _cumsum_rows(gt, reverse):
    t_row = _tri(not reverse).astype(BF16)
    lf = _log_sigmoid(gt)
    l0, l1, l2 = _split3(lf)
    return lf, _dot(l0, t_row) + _dot(l1, t_row) + _dot(l2, t_row)


def _twice(x):
    return jnp.concatenate([x, x], axis=-1)


def _chunk_step(q, kt, vaug, causal, lf_row, brow, li_row, c_aug, m, with_output):
    rowterm = li_row - brow
    b_end = jnp.sum(lf_row, axis=-1, keepdims=True)
    h = None
    if with_output:
        bcol = jnp.sum(jnp.where(causal, lf_row, 0.0), axis=-1, keepdims=True)
        dmat = jnp.where(causal, bcol + rowterm, -jnp.inf)
        inter = bcol + m
        m_t = jnp.maximum(inter, jnp.max(dmat, axis=-1, keepdims=True))
        w_inter = jnp.exp(inter - m_t)
        s = _dot(q, kt) * jnp.exp(dmat - m_t)
        nd = _twice(w_inter) * _dot(q, c_aug.astype(BF16)) + _dot(s.astype(BF16), vaug)
        h = nd[:, :M_HEAD_DIM] / jnp.maximum(jnp.abs(nd[:, M_HEAD_DIM:]), jnp.exp(-m_t))
    g = b_end + rowterm
    m_new = jnp.maximum(b_end + m, jnp.max(g, axis=-1, keepdims=True))
    w_old = jnp.exp(b_end + m - m_new)
    kw = (kt.astype(F32) * jnp.exp(g - m_new)).astype(BF16)
    return h, _twice(w_old) * c_aug + _dot(kw, vaug), m_new


def _scan_kernel(qf_ref, ktf_ref, vf_ref, gtf_ref, qb_ref, ktb_ref, vb_ref, gtb_ref,
                 ktc_ref, vc_ref, gtc_ref, hf_ref, hb_ref, c_scr, m_scr, *, ctx_chunks):
    c = pl.program_id(1)
    ones = jnp.ones((CHUNK, M_HEAD_DIM), BF16)

    def run_chunk(q, kt, v, gt, d, with_output):
        reverse = d == 1
        causal = _tri(reverse)
        lf, brow = _cumsum_rows(gt, reverse)
        outs = []
        for hd in range(M_HEADS):
            sl = slice(hd * M_HEAD_DIM, (hd + 1) * M_HEAD_DIM)
            gi, gf = d * 8 + hd, d * 8 + 4 + hd
            vaug = jnp.concatenate([v[:, sl], ones], axis=-1)
            r = d * M_HEADS + hd
            h, c_new, m_new = _chunk_step(
                q[:, sl] if with_output else None, kt[sl, :], vaug, causal,
                lf[gf:gf + 1, :], brow[gf:gf + 1, :], gt[gi:gi + 1, :],
                c_scr[r], m_scr[r, 0:1, :], with_output)
            c_scr[r] = c_new
            m_scr[r] = jnp.broadcast_to(m_new, (8, 128))
            outs.append(h)
        return outs

    @pl.when(c == 0)
    def _():
        c_scr[...] = jnp.zeros_like(c_scr)
        m_scr[...] = jnp.zeros_like(m_scr)
        for d in range(2):
            order = range(ctx_chunks) if d == 0 else range(ctx_chunks - 1, -1, -1)
            for j in order:
                rs = slice(j * CHUNK, (j + 1) * CHUNK)
                run_chunk(None, ktc_ref[0, :, rs], vc_ref[rs, :], gtc_ref[:, rs], d, False)

    hf = run_chunk(qf_ref[...], ktf_ref[0], vf_ref[...], gtf_ref[...], 0, True)
    hb = run_chunk(qb_ref[...], ktb_ref[0], vb_ref[...], gtb_ref[...], 1, True)
    for hd in range(M_HEADS):
        sl = slice(hd * M_HEAD_DIM, (hd + 1) * M_HEAD_DIM)
        hf_ref[:, sl] = hf[hd].astype(BF16)
        hb_ref[:, sl] = hb[hd].astype(BF16)


def _scan(q, kt, v, gt, kt_c, v_c, gt_c, seq_len, ctx_len):
    n = q.shape[0]
    nb = n // seq_len
    nc = seq_len // CHUNK
    fwd = lambda b, c: (b * nc + c, 0)
    bwd = lambda b, c: (b * nc + nc - 1 - c, 0)
    fwd_t = lambda b, c: (0, b * nc + c)
    bwd_t = lambda b, c: (0, b * nc + nc - 1 - c)
    tokb = lambda im: pl.BlockSpec((CHUNK, M_W), im)

    def side(tok_map, tok_map_t, kt_map):
        return [tokb(tok_map), pl.BlockSpec((1, M_W, CHUNK), kt_map), tokb(tok_map),
                pl.BlockSpec((N_GATES, CHUNK), tok_map_t)]

    in_specs = (side(fwd, fwd_t, lambda b, c: (b, 0, c))
                + side(bwd, bwd_t, lambda b, c: (b, 0, nc - 1 - c))
                + [pl.BlockSpec((1, M_W, ctx_len), lambda b, c: (b, 0, 0)),
                   pl.BlockSpec((ctx_len, M_W), lambda b, c: (b, 0)),
                   pl.BlockSpec((N_GATES, ctx_len), lambda b, c: (0, b))])
    return pl.pallas_call(
        functools.partial(_scan_kernel, ctx_chunks=ctx_len // CHUNK),
        grid=(nb, nc),
        in_specs=in_specs,
        out_specs=[tokb(fwd), tokb(bwd)],
        out_shape=[jax.ShapeDtypeStruct((n, M_W), BF16)] * 2,
        scratch_shapes=[pltpu.VMEM((2 * M_HEADS, M_HEAD_DIM, 2 * M_HEAD_DIM), F32),
                        pltpu.VMEM((2 * M_HEADS, 8, 128), F32)],
        compiler_params=_params(("parallel", "arbitrary")),
        name="mlstm_scan",
    )(q, kt, v, gt, q, kt, v, gt, kt_c, v_c, gt_c)


def _mixout_kernel(x_ref, mod_ref, p_ref, q_ref, hf_ref, hb_ref, a_ref, z_ref, mnw_ref, msk_ref,
                   wc_ref, wo_ref, n2w_ref, x1_ref, h2_ref):
    mod = mod_ref[0]
    gate1, shift2, scale2 = mod[2:3, :], mod[3:4, :], mod[4:5, :]
    yf = _dot(p_ref[...], wc_ref[0]) + _dot(q_ref[...], wc_ref[1])
    h = hf_ref[...].astype(F32) + hb_ref[...].astype(F32)
    parts = []
    for hd in range(M_HEADS):
        hh = h[:, hd * M_HEAD_DIM:(hd + 1) * M_HEAD_DIM]
        mu = jnp.mean(hh, axis=-1, keepdims=True)
        dlt = hh - mu
        var = jnp.mean(dlt * dlt, axis=-1, keepdims=True)
        parts.append(dlt * lax.rsqrt(var + NORM_EPS))
    hn = jnp.concatenate(parts, axis=-1) * mnw_ref[...]
    ym = (hn + msk_ref[...] * a_ref[...].astype(F32)) * _silu(z_ref[...].astype(F32))
    y = _dot(yf.astype(BF16), wo_ref[0]) + _dot(ym.astype(BF16), wo_ref[1])
    x1 = x_ref[...] + gate1 * y
    x1_ref[...] = x1
    h2_ref[...] = (_rms(x1, n2w_ref[...]) * (1.0 + scale2) + shift2).astype(BF16)


def _channel_dft():
    k = np.arange(F_GROUP_W, dtype=np.int64)
    ang = 2.0 * np.pi * ((k[:, None] * k[None, :]) % F_GROUP_W).astype(np.float64) / F_GROUP_W
    scale = 1.0 / math.sqrt(F_GROUP_W)
    eye = np.eye(F_GROUPS)
    wc = np.stack([np.kron(eye, np.cos(ang) * scale), -np.kron(eye, np.sin(ang) * scale)])
    return jnp.asarray(wc, dtype=F32).astype(BF16)


def _mixout(x2d, mod, p, q, hf, hb, a, z, mnorm_w, m_skip, w_out, norm2_w, seq_len, tm):
    n, d = x2d.shape
    tiles_per_seq = seq_len // tm
    tok = lambda i: (i, 0)
    half = pl.BlockSpec((tm, F_W), tok)
    full = pl.BlockSpec((tm, d), tok)
    vec = lambda w: pl.BlockSpec((1, w), lambda i: (0, 0))
    return pl.pallas_call(
        _mixout_kernel,
        grid=(n // tm,),
        in_specs=[full, pl.BlockSpec((1, 6, d), lambda i: (i // tiles_per_seq, 0, 0)),
                  half, half, half, half, half, half, vec(M_W), vec(M_W),
                  pl.BlockSpec((2, F_W, F_W), lambda i: (0, 0, 0)),
                  pl.BlockSpec((2, F_W, d), lambda i: (0, 0, 0)),
                  vec(d)],
        out_specs=[full, full],
        out_shape=[jax.ShapeDtypeStruct((n, d), F32), jax.ShapeDtypeStruct((n, d), BF16)],
        compiler_params=_params(("parallel",)),
        name="mixout",
    )(x2d, mod, p, q, hf, hb, a, z, mnorm_w.reshape(1, M_W), m_skip.reshape(1, M_W),
      _channel_dft(), w_out.astype(BF16).reshape(2, F_W, d), norm2_w.reshape(1, d))


def _ffn_kernel(h_ref, hp_ref, hn_ref, x1_ref, mod_ref, wu_ref, bu_ref, cw_ref, cb_ref, wd_ref, bd_ref,
                fw_ref, o_ref, hcat_scr, u0_scr, u1_scr, act_scr, acc_ref, *, rows, tiles, cw):
    t = pl.program_id(1)
    nt = rows * GRID_W
    nj = D_FF // cw
    hcat_scr[0:GRID_W] = hp_ref[...]
    hcat_scr[GRID_W:GRID_W + nt] = h_ref[...]
    hcat_scr[GRID_W + nt:] = hn_ref[...]
    col = lax.broadcasted_iota(jnp.int32, (GRID_W, 1), 0)
    first_col, last_col = col == 0, col == GRID_W - 1
    acc_ref[...] = jnp.zeros_like(acc_ref)

    def up(j, u_scr):
        u = _dot(hcat_scr[...], wu_ref[j]) + bu_ref[j]
        u_scr[0:GRID_W] = jnp.where(t > 0, u[0:GRID_W], 0.0)
        u_scr[GRID_W:GRID_W + nt] = u[GRID_W:GRID_W + nt]
        u_scr[GRID_W + nt:] = jnp.where(t < tiles - 1, u[GRID_W + nt:], 0.0)

    def consume(j, u_scr):
        w = cw_ref[j]
        cb = cb_ref[j]
        for r in range(rows):
            pcs = [u_scr[(r + dr) * GRID_W:(r + dr + 1) * GRID_W, :] for dr in range(3)]
            left = w[0:1, :] * pcs[0] + w[3:4, :] * pcs[1] + w[6:7, :] * pcs[2]
            mid = w[1:2, :] * pcs[0] + w[4:5, :] * pcs[1] + w[7:8, :] * pcs[2]
            right = w[2:3, :] * pcs[0] + w[5:6, :] * pcs[1] + w[8:9, :] * pcs[2]
            conv = (cb + mid + jnp.where(first_col, 0.0, pltpu.roll(left, 1, 0))
                    + jnp.where(last_col, 0.0, pltpu.roll(right, GRID_W - 1, 0)))
            act_scr[r * GRID_W:(r + 1) * GRID_W, :] = (conv[:, :cw] * _silu(conv[:, cw:])).astype(BF16)
        acc_ref[...] += _dot(act_scr[...], wd_ref[j])

    up(0, u0_scr)

    def body(jj, carry):
        j = 2 * jj
        up(j + 1, u1_scr)
        consume(j, u0_scr)
        up(j + 2, u0_scr)
        consume(j + 1, u1_scr)
        return carry

    lax.fori_loop(0, nj // 2 - 1, body, 0)
    up(nj - 1, u1_scr)
    consume(nj - 2, u0_scr)
    consume(nj - 1, u1_scr)
    mod = mod_ref[0]
    y = acc_ref[...] + bd_ref[...]
    o_ref[...] = _rms(x1_ref[...] + mod[5:6, :] * y, fw_ref[...])


def _ffn(h2, x1, mod, w_up, b_up, fconv_w, fconv_b, w_down, b_down, final_norm_w, seq_len, rows, cw):
    n, d = x1.shape
    nb = n // seq_len
    grid_h = seq_len // GRID_W
    tiles = grid_h // rows
    nt = rows * GRID_W
    nj = D_FF // cw

    def pair(w):
        lead = w.shape[:-1]
        w2 = w.reshape(lead + (2, nj, cw))
        w2 = jnp.moveaxis(w2, -2, 0)
        return w2.reshape((nj,) + lead + (2 * cw,))

    wu = pair(w_up.astype(BF16))
    bu = pair(b_up.reshape(1, 2 * D_FF))
    cwt = pair(fconv_w.reshape(9, 2 * D_FF))
    cbt = pair(fconv_b.reshape(1, 2 * D_FF))
    wd = w_down.astype(BF16).reshape(nj, cw, d)
    nblk = n // GRID_W
    const3 = lambda b, t: (0, 0, 0)
    return pl.pallas_call(
        functools.partial(_ffn_kernel, rows=rows, tiles=tiles, cw=cw),
        grid=(nb, tiles),
        in_specs=[pl.BlockSpec((nt, d), lambda b, t: (b * tiles + t, 0)),
                  pl.BlockSpec((GRID_W, d), lambda b, t: (jnp.maximum((b * tiles + t) * rows - 1, 0), 0)),
                  pl.BlockSpec((GRID_W, d), lambda b, t: (jnp.minimum((b * tiles + t + 1) * rows, nblk - 1), 0)),
                  pl.BlockSpec((nt, d), lambda b, t: (b * tiles + t, 0)),
                  pl.BlockSpec((1, 6, d), lambda b, t: (b, 0, 0)),
                  pl.BlockSpec((nj, d, 2 * cw), const3),
                  pl.BlockSpec((nj, 1, 2 * cw), const3),
                  pl.BlockSpec((nj, 9, 2 * cw), const3),
                  pl.BlockSpec((nj, 1, 2 * cw), const3),
                  pl.BlockSpec((nj, cw, d), const3),
                  pl.BlockSpec((1, d), lambda b, t: (0, 0)),
                  pl.BlockSpec((1, d), lambda b, t: (0, 0))],
        out_specs=pl.BlockSpec((nt, d), lambda b, t: (b * tiles + t, 0)),
        out_shape=jax.ShapeDtypeStruct((n, d), F32),
        scratch_shapes=[pltpu.VMEM((nt + 2 * GRID_W, d), BF16),
                        pltpu.VMEM((nt + 2 * GRID_W, 2 * cw), F32),
                        pltpu.VMEM((nt + 2 * GRID_W, 2 * cw), F32),
                        pltpu.VMEM((nt, cw), BF16),
                        pltpu.VMEM((nt, d), F32)],
        compiler_params=_params(("parallel", "parallel"), flags=FFN_FLAGS),
        name="conv_ffn",
    )(h2, h2, h2, x1, mod, wu, bu, cwt, cbt, wd, b_down.reshape(1, d), final_norm_w.reshape(1, d))


def kernel(x, c, ctx, c_ctx, w_ada, b_ada, norm1_w, w_in, mconv_w, mconv_b, w_q, w_k, b_gate, mnorm_w, m_skip,
           w_out, norm2_w, w_up, b_up, fconv_w, fconv_b, w_down, b_down, final_norm_w):
    bsz, seq_len, d = x.shape
    ctx_len = ctx.shape[1]
    assert w_ada.shape[0] == 1, "single-layer kernel"
    cond = jnp.concatenate([c, c_ctx[None, :], jnp.zeros((16 - bsz - 1, d), F32)], axis=0)
    mod = _adaln(cond, w_ada[0], b_ada[0]).reshape(16, 6, d)

    x2d = x.reshape(bsz * seq_len, d)
    ctx2d = ctx.reshape(bsz * ctx_len, d)

    xm_c, v_c, gt_c = _inproj(ctx2d, mod, lambda i: bsz, norm1_w[0], w_in[0], b_gate[0],
                              ctx_len, ctx_len, False)
    _, _, kt_c = _prep(xm_c, mconv_w[0], mconv_b[0], w_q[0], w_k[0], ctx_len, ctx_len)

    tm = 512
    xm, v, z, gt, u_perm = _inproj(x2d, mod, lambda i: i // (seq_len // tm), norm1_w[0], w_in[0],
                                   b_gate[0], seq_len, tm, True)
    p, q_im = _fourier(u_perm, seq_len)
    a, q, kt = _prep(xm, mconv_w[0], mconv_b[0], w_q[0], w_k[0], seq_len, 512)
    hf, hb = _scan(q, kt, v, gt, kt_c, v_c, gt_c, seq_len, ctx_len)
    x1, h2 = _mixout(x2d, mod, p, q_im, hf, hb, a, z, mnorm_w[0], m_skip[0], w_out[0], norm2_w[0],
                     seq_len, tm)
    out = _ffn(h2, x1, mod, w_up[0], b_up[0], fconv_w[0], fconv_b[0], w_down[0], b_down[0],
               final_norm_w, seq_len, 8, 256)
    return out.reshape(bsz, seq_len, d)
```

```python
import functools
import math

import numpy as np
import jax
import jax.numpy as jnp
from jax import lax
from jax.experimental import pallas as pl
from jax.experimental.pallas import tpu as pltpu

F32 = jnp.float32
BF16 = jnp.bfloat16

D_MODEL = 1024
GRID_W = 64
F_W = 512
F_GROUPS = 4
F_GROUP_W = F_W // F_GROUPS
M_W = 512
M_HEADS = 4
M_HEAD_DIM = M_W // M_HEADS
N_GATES = 16
MLSTM_END = F_W + 2 * M_W + N_GATES
CHUNK = 128
D_FF = 5 * D_MODEL // 2
NORM_EPS = 1e-6

DFT_RADIX = 8
VMEM_LIMIT = 56 * 1024 * 1024
SCAN_SEQS = 4
FFN_GRID_ROWS = 16
FFN_CHANNELS = 256
FFN_UP_ROWS = 128
FFN_DOWN_ROWS = 256


def _params(sem, vmem=VMEM_LIMIT):
    return pltpu.CompilerParams(dimension_semantics=sem, vmem_limit_bytes=vmem)


def _split3(a):
    hi = a.astype(BF16)
    r1 = a - hi.astype(F32)
    mid = r1.astype(BF16)
    lo = (r1 - mid.astype(F32)).astype(BF16)
    return hi, mid, lo


def _dot(a, b):
    return jnp.dot(a, b, preferred_element_type=F32)


def _dot_f32(a, b):
    a0, a1, a2 = _split3(a)
    b0, b1, b2 = _split3(b)
    return (_dot(a0, b0) + (_dot(a0, b1) + _dot(a1, b0))
            + (_dot(a1, b1) + _dot(a0, b2) + _dot(a2, b0)))


def _silu(x):
    return x * jax.nn.sigmoid(x)


def _rms(x, w):
    return x * lax.rsqrt(jnp.mean(x * x, axis=-1, keepdims=True) + NORM_EPS) * w


def _adaln_kernel(cond_ref, w_ref, b_ref, o_ref):
    s = _silu(cond_ref[...])
    o_ref[...] = _dot_f32(s, w_ref[...]) + b_ref[...]


def _adaln(cond, w_ada, b_ada):
    rows, d = cond.shape
    n = w_ada.shape[1]
    bn = 1536
    return pl.pallas_call(
        _adaln_kernel,
        grid=(n // bn,),
        in_specs=[pl.BlockSpec((rows, d), lambda j: (0, 0)),
                  pl.BlockSpec((d, bn), lambda j: (0, j)),
                  pl.BlockSpec((1, bn), lambda j: (0, j))],
        out_specs=pl.BlockSpec((rows, bn), lambda j: (0, j)),
        out_shape=jax.ShapeDtypeStruct((rows, n), F32),
        compiler_params=_params(("parallel",)),
        name="adaln",
    )(cond, w_ada, b_ada.reshape(1, n))


def _inproj_kernel(x_ref, mod_ref, nw_ref, wm_ref, wgt_ref, bgt_ref, *rest, tm, fourier):
    if fourier:
        wf_ref, xm_ref, v_ref, z_ref, gt_ref, u_ref, h_scr = rest
    else:
        xm_ref, v_ref, gt_ref = rest
    mod = mod_ref[0]
    shift, scale = mod[0:1, :], mod[1:2, :]
    h = _rms(x_ref[...], nw_ref[...]) * (1.0 + scale) + shift
    hb = h.astype(BF16)
    p = _dot(hb, wm_ref[...])
    xm_ref[...] = p[:, 0:M_W].astype(BF16)
    v_ref[...] = p[:, M_W:2 * M_W].astype(BF16)
    gt_ref[0] = lax.dot_general(wgt_ref[...], hb, (((1,), (1,)), ((), ())),
                                preferred_element_type=F32) + bgt_ref[...]
    if fourier:
        z_ref[...] = p[:, 2 * M_W:3 * M_W].astype(BF16)
        nlb = h.shape[1] // 128
        for lb in range(nlb):
            h_scr[lb] = h[:, lb * 128:(lb + 1) * 128]
        sub = tm // DFT_RADIX
        hs = jnp.concatenate(
            [jnp.concatenate([h_scr[lb, pl.ds(n1, sub, stride=DFT_RADIX), :] for lb in range(nlb)], axis=1)
             for n1 in range(DFT_RADIX)], axis=0)
        uf = _dot(hs.astype(BF16), wf_ref[...])
        for n1 in range(DFT_RADIX):
            u_ref[0, n1] = uf[n1 * sub:(n1 + 1) * sub, :].astype(BF16)


def _inproj(x2d, mod, mod_row, norm_w, w_in, b_gate, seq_len, tm, fourier):
    n, d = x2d.shape
    tiles_per_seq = seq_len // tm
    nb = n // seq_len
    w_in_b = w_in.astype(BF16)
    w_xm_v = w_in_b[:, F_W:F_W + 2 * M_W]
    w_gt = w_in_b[:, F_W + 2 * M_W:MLSTM_END].T
    if fourier:
        w_main = jnp.concatenate([w_xm_v, w_in_b[:, MLSTM_END:]], axis=1)
    else:
        w_main = w_xm_v
    nm = w_main.shape[1]
    tok = lambda i: (i, 0)
    const = lambda i: (0, 0)
    in_specs = [pl.BlockSpec((tm, d), tok),
                pl.BlockSpec((1, 6, d), lambda i: (mod_row(i), 0, 0)),
                pl.BlockSpec((1, d), const),
                pl.BlockSpec((d, nm), const),
                pl.BlockSpec((N_GATES, d), const),
                pl.BlockSpec((N_GATES, 1), const)]
    args = [x2d, mod, norm_w.reshape(1, d), w_main, w_gt, b_gate.reshape(N_GATES, 1)]
    tokb = pl.BlockSpec((tm, M_W), tok)
    out_specs = [tokb, tokb]
    out_shape = [jax.ShapeDtypeStruct((n, M_W), BF16)] * 2
    if fourier:
        in_specs.append(pl.BlockSpec((d, F_W), const))
        args.append(w_in_b[:, :F_W])
        out_specs.append(tokb)
        out_shape.append(jax.ShapeDtypeStruct((n, M_W), BF16))
    out_specs.append(pl.BlockSpec((1, N_GATES, tm), lambda i: (i // tiles_per_seq, 0, i % tiles_per_seq)))
    out_shape.append(jax.ShapeDtypeStruct((nb, N_GATES, seq_len), F32))
    scratch = []
    if fourier:
        sub = tm // DFT_RADIX
        out_specs.append(pl.BlockSpec((1, DFT_RADIX, sub, F_W),
                                      lambda i: (i // tiles_per_seq, 0, i % tiles_per_seq, 0)))
        out_shape.append(jax.ShapeDtypeStruct((nb, DFT_RADIX, seq_len // DFT_RADIX, F_W), BF16))
        scratch.append(pltpu.VMEM((d // 128, tm, 128), F32))
    return pl.pallas_call(
        functools.partial(_inproj_kernel, tm=tm, fourier=fourier),
        grid=(n // tm,),
        in_specs=in_specs, out_specs=out_specs, out_shape=out_shape,
        scratch_shapes=scratch,
        compiler_params=_params(("parallel",)),
        name="inproj_latent" if fourier else "inproj_ctx",
    )(*args)


def _dft_tables(seq_len):
    sub = seq_len // DFT_RADIX
    k2 = np.arange(sub, dtype=np.int64)[None, :, None]
    n1 = np.arange(DFT_RADIX, dtype=np.int64)[:, None, None]
    n2 = np.arange(sub, dtype=np.int64)[None, None, :]
    ang = 2.0 * np.pi * ((k2 * (n1 + DFT_RADIX * n2)) % seq_len).astype(np.float64) / seq_len
    scale = 1.0 / math.sqrt(seq_len)
    return (jnp.asarray(np.cos(ang) * scale, dtype=F32).astype(BF16),
            jnp.asarray(np.sin(ang) * scale, dtype=F32).astype(BF16))


def _cadd(a, b):
    return (a[0] + b[0], a[1] + b[1])


def _csub(a, b):
    return (a[0] - b[0], a[1] - b[1])


def _cmul_w8(z, k):
    re, im = z
    r = math.sqrt(0.5)
    if k == 0:
        return z
    if k == 1:
        return ((re + im) * r, (im - re) * r)
    if k == 2:
        return (im, -re)
    return ((im - re) * r, (-re - im) * r)


def _dft8(z):
    def dft4(a, b, c, d):
        s0, s1 = _cadd(a, c), _csub(a, c)
        t0, t1 = _cadd(b, d), _csub(b, d)
        t1r = _cmul_w8(t1, 2)
        return [_cadd(s0, t0), _cadd(s1, t1r), _csub(s0, t0), _csub(s1, t1r)]
    ev = dft4(z[0], z[2], z[4], z[6])
    od = dft4(z[1], z[3], z[5], z[7])
    out = [None] * 8
    for k in range(4):
        t = _cmul_w8(od[k], k)
        out[k] = _cadd(ev[k], t)
        out[k + 4] = _csub(ev[k], t)
    return out


def _fourier_kernel(u_ref, tc_ref, ts_ref, p_ref, q_ref, ga_ref, gb_ref, *, sub, rb):
    for n1 in range(DFT_RADIX):
        un = u_ref[0, n1]
        ga_ref[n1] = _dot(tc_ref[n1], un)
        gb_ref[n1] = _dot(ts_ref[n1], un)

    def body(i, carry):
        r0 = pl.multiple_of(i * rb, rb)
        z = [(ga_ref[n1, pl.ds(r0, rb), :], -gb_ref[n1, pl.ds(r0, rb), :]) for n1 in range(DFT_RADIX)]
        x = _dft8(z)
        for k1 in range(DFT_RADIX):
            p_ref[pl.ds(k1 * sub + r0, rb), :] = x[k1][0].astype(BF16)
            q_ref[pl.ds(k1 * sub + r0, rb), :] = (-x[k1][1]).astype(BF16)
        return carry

    lax.fori_loop(0, sub // rb, body, 0)


def _fourier(u_perm, seq_len):
    nb = u_perm.shape[0]
    sub = seq_len // DFT_RADIX
    cb = 256
    tc, ts = _dft_tables(seq_len)
    return pl.pallas_call(
        functools.partial(_fourier_kernel, sub=sub, rb=32),
        grid=(nb, F_W // cb),
        in_specs=[pl.BlockSpec((1, DFT_RADIX, sub, cb), lambda b, j: (b, 0, 0, j)),
                  pl.BlockSpec((DFT_RADIX, sub, sub), lambda b, j: (0, 0, 0)),
                  pl.BlockSpec((DFT_RADIX, sub, sub), lambda b, j: (0, 0, 0))],
        out_specs=[pl.BlockSpec((seq_len, cb), lambda b, j: (b, j))] * 2,
        out_shape=[jax.ShapeDtypeStruct((nb * seq_len, F_W), BF16)] * 2,
        scratch_shapes=[pltpu.VMEM((DFT_RADIX, sub, cb), F32)] * 2,
        compiler_params=_params(("parallel", "parallel")),
        name="fourier",
    )(u_perm, tc, ts)


def _prep_kernel(xm_ref, prev_ref, next_ref, cw_ref, cb_ref, wq_ref, wkt_ref, a_ref, q_ref, kt_ref,
                 *, t, tiles_per_seq):
    i = pl.program_id(0)
    first = (i % tiles_per_seq) == 0
    last = (i % tiles_per_seq) == tiles_per_seq - 1
    x = xm_ref[...].astype(F32)
    prev_row = jnp.where(first, 0.0, prev_ref[15:16, :].astype(F32))
    next_row = jnp.where(last, 0.0, next_ref[0:1, :].astype(F32))
    rows = lax.broadcasted_iota(jnp.int32, (t, 1), 0)
    xp = jnp.where(rows == 0, prev_row, pltpu.roll(x, 1, 0))
    xn = jnp.where(rows == t - 1, next_row, pltpu.roll(x, t - 1, 0))
    cw = cw_ref[...]
    a = _silu(cw[0:1, :] * xp + cw[1:2, :] * x + cw[2:3, :] * xn + cb_ref[...])
    ab = a.astype(BF16)
    a_ref[...] = ab
    for pair in range(2):
        sl = slice(pair * 256, (pair + 1) * 256)
        q_ref[:, sl] = _dot(ab[:, sl], wq_ref[pair]).astype(BF16)
        kt = lax.dot_general(wkt_ref[pair], ab[:, sl], (((1,), (1,)), ((), ())),
                             preferred_element_type=F32)
        kt_ref[0, sl, :] = (kt * M_HEAD_DIM ** -0.5).astype(BF16)


def _pair_blockdiag(w):
    z = jnp.zeros_like(w[0])
    return jnp.stack([jnp.block([[w[0], z], [z, w[1]]]), jnp.block([[w[2], z], [z, w[3]]])])


def _prep(xm, mconv_w, mconv_b, w_q, w_k, seq_len, t):
    n = xm.shape[0]
    nb = n // seq_len
    tiles_per_seq = seq_len // t
    hb = t // 16
    nhalo = n // 16
    wq_bd = _pair_blockdiag(w_q.astype(BF16))
    wkt_bd = _pair_blockdiag(jnp.swapaxes(w_k, 1, 2).astype(BF16))
    return pl.pallas_call(
        functools.partial(_prep_kernel, t=t, tiles_per_seq=tiles_per_seq),
        grid=(n // t,),
        in_specs=[pl.BlockSpec((t, M_W), lambda i: (i, 0)),
                  pl.BlockSpec((16, M_W), lambda i: (jnp.maximum(i * hb - 1, 0), 0)),
                  pl.BlockSpec((16, M_W), lambda i: (jnp.minimum((i + 1) * hb, nhalo - 1), 0)),
                  pl.BlockSpec((3, M_W), lambda i: (0, 0)),
                  pl.BlockSpec((1, M_W), lambda i: (0, 0)),
                  pl.BlockSpec((2, 256, 256), lambda i: (0, 0, 0)),
                  pl.BlockSpec((2, 256, 256), lambda i: (0, 0, 0))],
        out_specs=[pl.BlockSpec((t, M_W), lambda i: (i, 0)),
                   pl.BlockSpec((t, M_W), lambda i: (i, 0)),
                   pl.BlockSpec((1, M_W, t), lambda i: (i // tiles_per_seq, 0, i % tiles_per_seq))],
        out_shape=[jax.ShapeDtypeStruct((n, M_W), BF16),
                   jax.ShapeDtypeStruct((n, M_W), BF16),
                   jax.ShapeDtypeStruct((nb, M_W, seq_len), BF16)],
        compiler_params=_params(("parallel",)),
        name="prep_%d" % seq_len,
    )(xm, xm, xm, mconv_w, mconv_b.reshape(1, M_W), wq_bd, wkt_bd)


def _log_sigmoid(x):
    return jnp.minimum(x, 0.0) - jnp.log1p(jnp.exp(-jnp.abs(x)))


def _tri(reverse):
    r = lax.broadcasted_iota(jnp.int32, (CHUNK, CHUNK), 0)
    c = lax.broadcasted_iota(jnp.int32, (CHUNK, CHUNK), 1)
    return (c >= r) if reverse else (c <= r)


def _cumsum_rows(gt, reverse):
    t_row = _tri(not reverse).astype(BF16)
    lf = _log_sigmoid(gt)
    l0, l1, l2 = _split3(lf)
    return lf, _dot(l0, t_row) + _dot(l1, t_row) + _dot(l2, t_row)


def _twice(x):
    return jnp.concatenate([x, x], axis=-1)


def _chunk_step(q, kt, vaug, causal, lf_row, brow, li_row, c_aug, m, with_output):
    rowterm = li_row - brow
    b_end = jnp.sum(lf_row, axis=-1, keepdims=True)
    g = b_end + rowterm
    m_new = jnp.maximum(b_end + m, jnp.max(g, axis=-1, keepdims=True))
    w_old = jnp.exp(b_end + m - m_new)
    kw = (kt.astype(F32) * jnp.exp(g - m_new)).astype(BF16)
    if not with_output:
        return None, _twice(w_old) * c_aug + _dot(kw, vaug), m_new
    bcol = jnp.sum(jnp.where(causal, lf_row, 0.0), axis=-1, keepdims=True)
    dmat = jnp.where(causal, bcol + rowterm, -jnp.inf)
    inter = bcol + m
    m_t = jnp.maximum(inter, jnp.max(dmat, axis=-1, keepdims=True))
    w_inter = jnp.exp(inter - m_t)
    qkc = _dot(q, jnp.concatenate([kt, c_aug.astype(BF16)], axis=-1))
    s = qkc[:, :CHUNK] * jnp.exp(dmat - m_t)
    sv = _dot(jnp.concatenate([s.astype(BF16), kw], axis=0), vaug)
    nd = _twice(w_inter) * qkc[:, CHUNK:] + sv[:CHUNK]
    h = nd[:, :M_HEAD_DIM] / jnp.maximum(jnp.abs(nd[:, M_HEAD_DIM:]), jnp.exp(-m_t))
    return h, _twice(w_old) * c_aug + sv[CHUNK:], m_new


def _scan_kernel(qf_ref, ktf_ref, vf_ref, gtf_ref, qb_ref, ktb_ref, vb_ref, gtb_ref,
                 ktc_ref, vc_ref, gtc_ref, hf_ref, hb_ref, c_scr, m_scr, *, ctx_chunks, nbb):
    c = pl.program_id(1)
    ones = jnp.ones((CHUNK, M_HEAD_DIM), BF16)

    def run_chunk(bb, q, kt, v, gt, d, with_output):
        reverse = d == 1
        causal = _tri(reverse)
        lf, brow = _cumsum_rows(gt, reverse)
        outs = []
        for hd in range(M_HEADS):
            sl = slice(hd * M_HEAD_DIM, (hd + 1) * M_HEAD_DIM)
            gi, gf = d * 8 + hd, d * 8 + 4 + hd
            vaug = jnp.concatenate([v[:, sl], ones], axis=-1)
            r = (bb * 2 + d) * M_HEADS + hd
            h, c_new, m_new = _chunk_step(
                q[:, sl] if with_output else None, kt[sl, :], vaug, causal,
                lf[gf:gf + 1, :], brow[gf:gf + 1, :], gt[gi:gi + 1, :],
                c_scr[r], m_scr[r, 0:1, :], with_output)
            c_scr[r] = c_new
            m_scr[r] = jnp.broadcast_to(m_new, (8, 128))
            outs.append(h)
        return outs

    @pl.when(c == 0)
    def _():
        c_scr[...] = jnp.zeros_like(c_scr)
        m_scr[...] = jnp.zeros_like(m_scr)
        for bb in range(nbb):
            for d in range(2):
                order = range(ctx_chunks) if d == 0 else range(ctx_chunks - 1, -1, -1)
                for j in order:
                    rs = slice(j * CHUNK, (j + 1) * CHUNK)
                    run_chunk(bb, None, ktc_ref[bb, :, rs], vc_ref[bb, rs, :], gtc_ref[bb, :, rs], d, False)

    for bb in range(nbb):
        hf = run_chunk(bb, qf_ref[bb], ktf_ref[bb], vf_ref[bb], gtf_ref[bb], 0, True)
        hb = run_chunk(bb, qb_ref[bb], ktb_ref[bb], vb_ref[bb], gtb_ref[bb], 1, True)
        for hd in range(M_HEADS):
            sl = slice(hd * M_HEAD_DIM, (hd + 1) * M_HEAD_DIM)
            hf_ref[bb, :, sl] = hf[hd].astype(BF16)
            hb_ref[bb, :, sl] = hb[hd].astype(BF16)


def _scan(q, kt, v, gt, kt_c, v_c, gt_c, seq_len, ctx_len, nbb):
    n = q.shape[0]
    nb = n // seq_len
    nc = seq_len // CHUNK
    q3, v3 = q.reshape(nb, seq_len, M_W), v.reshape(nb, seq_len, M_W)
    fwd = lambda b, c: (b, c, 0)
    bwd = lambda b, c: (b, nc - 1 - c, 0)
    fwd_t = lambda b, c: (b, 0, c)
    bwd_t = lambda b, c: (b, 0, nc - 1 - c)
    tokb = lambda im: pl.BlockSpec((nbb, CHUNK, M_W), im)

    def side(tok_map, tok_map_t):
        return [tokb(tok_map), pl.BlockSpec((nbb, M_W, CHUNK), tok_map_t), tokb(tok_map),
                pl.BlockSpec((nbb, N_GATES, CHUNK), tok_map_t)]

    in_specs = (side(fwd, fwd_t) + side(bwd, bwd_t)
                + [pl.BlockSpec((nbb, M_W, ctx_len), lambda b, c: (b, 0, 0)),
                   pl.BlockSpec((nbb, ctx_len, M_W), lambda b, c: (b, 0, 0)),
                   pl.BlockSpec((nbb, N_GATES, ctx_len), lambda b, c: (b, 0, 0))])
    hf, hb = pl.pallas_call(
        functools.partial(_scan_kernel, ctx_chunks=ctx_len // CHUNK, nbb=nbb),
        grid=(nb // nbb, nc),
        in_specs=in_specs,
        out_specs=[tokb(fwd), tokb(bwd)],
        out_shape=[jax.ShapeDtypeStruct((nb, seq_len, M_W), BF16)] * 2,
        scratch_shapes=[pltpu.VMEM((nbb * 2 * M_HEADS, M_HEAD_DIM, 2 * M_HEAD_DIM), F32),
                        pltpu.VMEM((nbb * 2 * M_HEADS, 8, 128), F32)],
        compiler_params=_params(("parallel", "arbitrary")),
        name="mlstm_scan",
    )(q3, kt, v3, gt, q3, kt, v3, gt, kt_c, v_c.reshape(nb, ctx_len, M_W), gt_c)
    return hf.reshape(n, M_W), hb.reshape(n, M_W)


def _mixout_kernel(x_ref, mod_ref, p_ref, q_ref, hf_ref, hb_ref, a_ref, z_ref, mnw_ref, msk_ref,
                   wc_ref, wo_ref, n2w_ref, x1_ref, h2_ref):
    mod = mod_ref[0]
    gate1, shift2, scale2 = mod[2:3, :], mod[3:4, :], mod[4:5, :]
    yf = _dot(p_ref[...], wc_ref[0]) + _dot(q_ref[...], wc_ref[1])
    h = hf_ref[...].astype(F32) + hb_ref[...].astype(F32)
    parts = []
    for hd in range(M_HEADS):
        hh = h[:, hd * M_HEAD_DIM:(hd + 1) * M_HEAD_DIM]
        mu = jnp.mean(hh, axis=-1, keepdims=True)
        dlt = hh - mu
        var = jnp.mean(dlt * dlt, axis=-1, keepdims=True)
        parts.append(dlt * lax.rsqrt(var + NORM_EPS))
    hn = jnp.concatenate(parts, axis=-1) * mnw_ref[...]
    ym = (hn + msk_ref[...] * a_ref[...].astype(F32)) * _silu(z_ref[...].astype(F32))
    y = _dot(yf.astype(BF16), wo_ref[0]) + _dot(ym.astype(BF16), wo_ref[1])
    x1 = x_ref[...] + gate1 * y
    x1_ref[...] = x1
    h2_ref[...] = (_rms(x1, n2w_ref[...]) * (1.0 + scale2) + shift2).astype(BF16)


def _channel_dft():
    k = np.arange(F_GROUP_W, dtype=np.int64)
    ang = 2.0 * np.pi * ((k[:, None] * k[None, :]) % F_GROUP_W).astype(np.float64) / F_GROUP_W
    scale = 1.0 / math.sqrt(F_GROUP_W)
    eye = np.eye(F_GROUPS)
    wc = np.stack([np.kron(eye, np.cos(ang) * scale), -np.kron(eye, np.sin(ang) * scale)])
    return jnp.asarray(wc, dtype=F32).astype(BF16)


def _mixout(x2d, mod, p, q, hf, hb, a, z, mnorm_w, m_skip, w_out, norm2_w, seq_len, tm):
    n, d = x2d.shape
    tiles_per_seq = seq_len // tm
    tok = lambda i: (i, 0)
    half = pl.BlockSpec((tm, F_W), tok)
    full = pl.BlockSpec((tm, d), tok)
    vec = lambda w: pl.BlockSpec((1, w), lambda i: (0, 0))
    return pl.pallas_call(
        _mixout_kernel,
        grid=(n // tm,),
        in_specs=[full, pl.BlockSpec((1, 6, d), lambda i: (i // tiles_per_seq, 0, 0)),
                  half, half, half, half, half, half, vec(M_W), vec(M_W),
                  pl.BlockSpec((2, F_W, F_W), lambda i: (0, 0, 0)),
                  pl.BlockSpec((2, F_W, d), lambda i: (0, 0, 0)),
                  vec(d)],
        out_specs=[full, full],
        out_shape=[jax.ShapeDtypeStruct((n, d), F32), jax.ShapeDtypeStruct((n, d), BF16)],
        compiler_params=_params(("parallel",)),
        name="mixout",
    )(x2d, mod, p, q, hf, hb, a, z, mnorm_w.reshape(1, M_W), m_skip.reshape(1, M_W),
      _channel_dft(), w_out.astype(BF16).reshape(2, F_W, d), norm2_w.reshape(1, d))


def _zero_of(x):
    bits = pltpu.bitcast(x, jnp.uint32)
    return pltpu.bitcast(lax.shift_right_logical(lax.shift_right_logical(bits, jnp.uint32(16)), jnp.uint32(16)), F32)


def _ffn_kernel(h_ref, hp_ref, hn_ref, x1_ref, mod_ref, wu_ref, bu_ref, cw_ref, cb_ref, wd_ref, bd_ref,
                fw_ref, o_ref, hcat_scr, u0_scr, u1_scr, act0_scr, act1_scr, *, rows, tiles, cw):
    t = pl.program_id(1)
    nt = rows * GRID_W
    ne = nt + 2 * GRID_W
    nj = D_FF // cw
    n_uc = ne // FFN_UP_ROWS
    n_dc = nt // FFN_DOWN_ROWS
    hcat_scr[0:GRID_W] = hp_ref[...]
    hcat_scr[GRID_W:GRID_W + nt] = h_ref[...]
    hcat_scr[GRID_W + nt:] = hn_ref[...]
    row8 = lax.broadcasted_iota(jnp.int32, (8, 1), 0)
    o_ref[...] = jnp.zeros_like(o_ref)

    def up_chunk(j, u_scr, c, after=None):
        r0 = c * FFN_UP_ROWS
        lhs = hcat_scr[r0:r0 + FFN_UP_ROWS, :]
        if after is not None:
            lhs = lhs + _twice(_zero_of(after)).astype(BF16)
        u = _dot(lhs, wu_ref[j]) + bu_ref[j]
        if c == 0:
            u_scr[0:GRID_W] = jnp.where(t > 0, u[0:GRID_W], 0.0)
            u_scr[GRID_W:FFN_UP_ROWS] = u[GRID_W:]
        elif c == n_uc - 1:
            u_scr[r0:ne - GRID_W] = u[:FFN_UP_ROWS - GRID_W]
            u_scr[ne - GRID_W:] = jnp.where(t < tiles - 1, u[FFN_UP_ROWS - GRID_W:], 0.0)
        else:
            u_scr[r0:r0 + FFN_UP_ROWS] = u

    def conv_row(j, u_scr, a_scr, r):
        w = cw_ref[j]
        pcs = [u_scr[(r + dr) * GRID_W:(r + dr + 1) * GRID_W, :] for dr in range(3)]
        left = w[0:1, :] * pcs[0] + w[3:4, :] * pcs[1] + w[6:7, :] * pcs[2]
        mid = w[1:2, :] * pcs[0] + w[4:5, :] * pcs[1] + w[7:8, :] * pcs[2]
        right = w[2:3, :] * pcs[0] + w[5:6, :] * pcs[1] + w[8:9, :] * pcs[2]
        rl = pltpu.roll(left, 1, 0)
        rr = pltpu.roll(right, GRID_W - 1, 0)
        rl = jnp.concatenate([jnp.where(row8 == 0, 0.0, rl[0:8]), rl[8:]], axis=0)
        rr = jnp.concatenate([rr[:GRID_W - 8], jnp.where(row8 == 7, 0.0, rr[GRID_W - 8:])], axis=0)
        conv = cb_ref[j] + mid + rl + rr
        a_scr[r * GRID_W:(r + 1) * GRID_W, :] = (conv[:, :cw] * _silu(conv[:, cw:])).astype(BF16)
        return conv[0:1, :]

    def down_chunk(j, a_scr, c, after=None):
        rs = slice(c * FFN_DOWN_ROWS, (c + 1) * FFN_DOWN_ROWS)
        lhs = a_scr[rs, :]
        if after is not None:
            lhs = lhs + _zero_of(after)[:, :cw].astype(BF16)
        o_ref[rs, :] += _dot(lhs, wd_ref[j])

    def stage(up=None, conv=None, down=None):
        mm = []
        if up is not None:
            mm += [functools.partial(up_chunk, *up, c) for c in range(n_uc)]
        if down is not None:
            mm += [functools.partial(down_chunk, *down, c) for c in range(n_dc)]
        if conv is None:
            for fn in mm:
                fn(after=None)
            return
        gate = {}
        for k in range(1, len(mm)):
            gate.setdefault(min((k * rows) // len(mm), rows - 1), []).append(k)
        if mm:
            mm[0](after=None)
        for r in range(rows):
            tag = conv_row(*conv, r)
            for k in gate.get(r, []):
                mm[k](after=tag)

    stage(up=(0, u0_scr))
    stage(up=(1, u1_scr), conv=(0, u0_scr, act0_scr))

    def body(jj, carry):
        j = 2 * jj + 1
        stage(up=(j + 1, u0_scr), conv=(j, u1_scr, act1_scr), down=(j - 1, act0_scr))
        stage(up=(j + 2, u1_scr), conv=(j + 1, u0_scr, act0_scr), down=(j, act1_scr))
        return carry

    lax.fori_loop(0, (nj - 2) // 2, body, 0)
    stage(conv=(nj - 1, u1_scr, act1_scr), down=(nj - 2, act0_scr))
    stage(down=(nj - 1, act1_scr))
    mod = mod_ref[0]
    y = o_ref[...] + bd_ref[...]
    o_ref[...] = _rms(x1_ref[...] + mod[5:6, :] * y, fw_ref[...])


def _ffn(h2, x1, mod, w_up, b_up, fconv_w, fconv_b, w_down, b_down, final_norm_w, seq_len, rows, cw):
    n, d = x1.shape
    nb = n // seq_len
    grid_h = seq_len // GRID_W
    tiles = grid_h // rows
    nt = rows * GRID_W
    nj = D_FF // cw

    def pair(w):
        lead = w.shape[:-1]
        w2 = w.reshape(lead + (2, nj, cw))
        w2 = jnp.moveaxis(w2, -2, 0)
        return w2.reshape((nj,) + lead + (2 * cw,))

    wu = pair(w_up.astype(BF16))
    bu = pair(b_up.reshape(1, 2 * D_FF))
    cwt = pair(fconv_w.reshape(9, 2 * D_FF))
    cbt = pair(fconv_b.reshape(1, 2 * D_FF))
    wd = w_down.astype(BF16).reshape(nj, cw, d)
    nblk = n // GRID_W
    const3 = lambda b, t: (0, 0, 0)
    return pl.pallas_call(
        functools.partial(_ffn_kernel, rows=rows, tiles=tiles, cw=cw),
        grid=(nb, tiles),
        in_specs=[pl.BlockSpec((nt, d), lambda b, t: (b * tiles + t, 0)),
                  pl.BlockSpec((GRID_W, d), lambda b, t: (jnp.maximum((b * tiles + t) * rows - 1, 0), 0)),
                  pl.BlockSpec((GRID_W, d), lambda b, t: (jnp.minimum((b * tiles + t + 1) * rows, nblk - 1), 0)),
                  pl.BlockSpec((nt, d), lambda b, t: (b * tiles + t, 0)),
                  pl.BlockSpec((1, 6, d), lambda b, t: (b, 0, 0)),
                  pl.BlockSpec((nj, d, 2 * cw), const3, pipeline_mode=pl.Buffered(1)),
                  pl.BlockSpec((nj, 1, 2 * cw), const3),
                  pl.BlockSpec((nj, 9, 2 * cw), const3),
                  pl.BlockSpec((nj, 1, 2 * cw), const3),
                  pl.BlockSpec((nj, cw, d), const3, pipeline_mode=pl.Buffered(1)),
                  pl.BlockSpec((1, d), lambda b, t: (0, 0)),
                  pl.BlockSpec((1, d), lambda b, t: (0, 0))],
        out_specs=pl.BlockSpec((nt, d), lambda b, t: (b * tiles + t, 0)),
        out_shape=jax.ShapeDtypeStruct((n, d), F32),
        scratch_shapes=[pltpu.VMEM((nt + 2 * GRID_W, d), BF16),
                        pltpu.VMEM((nt + 2 * GRID_W, 2 * cw), F32),
                        pltpu.VMEM((nt + 2 * GRID_W, 2 * cw), F32),
                        pltpu.VMEM((nt, cw), BF16),
                        pltpu.VMEM((nt, cw), BF16)],
        compiler_params=_params(("parallel", "parallel")),
        name="conv_ffn",
    )(h2, h2, h2, x1, mod, wu, bu, cwt, cbt, wd, b_down.reshape(1, d), final_norm_w.reshape(1, d))


def kernel(x, c, ctx, c_ctx, w_ada, b_ada, norm1_w, w_in, mconv_w, mconv_b, w_q, w_k, b_gate, mnorm_w, m_skip,
           w_out, norm2_w, w_up, b_up, fconv_w, fconv_b, w_down, b_down, final_norm_w):
    bsz, seq_len, d = x.shape
    ctx_len = ctx.shape[1]
    assert w_ada.shape[0] == 1, "single-layer kernel"
    cond = jnp.concatenate([c, c_ctx[None, :], jnp.zeros((16 - bsz - 1, d), F32)], axis=0)
    mod = _adaln(cond, w_ada[0], b_ada[0]).reshape(16, 6, d)

    x2d = x.reshape(bsz * seq_len, d)
    ctx2d = ctx.reshape(bsz * ctx_len, d)

    xm_c, v_c, gt_c = _inproj(ctx2d, mod, lambda i: bsz, norm1_w[0], w_in[0], b_gate[0],
                              ctx_len, ctx_len, False)
    _, _, kt_c = _prep(xm_c, mconv_w[0], mconv_b[0], w_q[0], w_k[0], ctx_len, ctx_len)

    tm = 512
    xm, v, z, gt, u_perm = _inproj(x2d, mod, lambda i: i // (seq_len // tm), norm1_w[0], w_in[0],
                                   b_gate[0], seq_len, tm, True)
    p, q_im = _fourier(u_perm, seq_len)
    a, q, kt = _prep(xm, mconv_w[0], mconv_b[0], w_q[0], w_k[0], seq_len, 512)
    hf, hb = _scan(q, kt, v, gt, kt_c, v_c, gt_c, seq_len, ctx_len, SCAN_SEQS)
    x1, h2 = _mixout(x2d, mod, p, q_im, hf, hb, a, z, mnorm_w[0], m_skip[0], w_out[0], norm2_w[0],
                     seq_len, tm)
    out = _ffn(h2, x1, mod, w_up[0], b_up[0], fconv_w[0], fconv_b[0], w_down[0], b_down[0],
               final_norm_w, seq_len, FFN_GRID_ROWS, FFN_CHANNELS)
    return out.reshape(bsz, seq_len, d)
```

```python
import functools
import math

import numpy as np
import jax
import jax.numpy as jnp
from jax import lax
from jax.experimental import pallas as pl
from jax.experimental.pallas import tpu as pltpu

F32 = jnp.float32
BF16 = jnp.bfloat16

D_MODEL = 1024
GRID_W = 64
F_W = 512
F_GROUPS = 4
F_GROUP_W = F_W // F_GROUPS
M_W = 512
M_HEADS = 4
M_HEAD_DIM = M_W // M_HEADS
N_GATES = 16
MLSTM_END = F_W + 2 * M_W + N_GATES
CHUNK = 128
D_FF = 5 * D_MODEL // 2
NORM_EPS = 1e-6

DFT_RADIX = 8
VMEM_LIMIT = 56 * 1024 * 1024
SCAN_SEQS = 4
FFN_GRID_ROWS = 8
FFN_CHANNELS = 256
FFN_UP_ROWS = 128
FFN_DOWN_ROWS = 256


def _params(sem, vmem=VMEM_LIMIT):
    return pltpu.CompilerParams(dimension_semantics=sem, vmem_limit_bytes=vmem)


def _split3(a):
    hi = a.astype(BF16)
    r1 = a - hi.astype(F32)
    mid = r1.astype(BF16)
    lo = (r1 - mid.astype(F32)).astype(BF16)
    return hi, mid, lo


def _dot(a, b):
    return jnp.dot(a, b, preferred_element_type=F32)


def _dot_f32(a, b):
    a0, a1, a2 = _split3(a)
    b0, b1, b2 = _split3(b)
    return (_dot(a0, b0) + (_dot(a0, b1) + _dot(a1, b0))
            + (_dot(a1, b1) + _dot(a0, b2) + _dot(a2, b0)))


def _silu(x):
    return x * jax.nn.sigmoid(x)


def _rms(x, w):
    return x * lax.rsqrt(jnp.mean(x * x, axis=-1, keepdims=True) + NORM_EPS) * w


def _adaln_kernel(cond_ref, w_ref, b_ref, o_ref):
    s = _silu(cond_ref[...])
    o_ref[...] = _dot_f32(s, w_ref[...]) + b_ref[...]


def _adaln(cond, w_ada, b_ada):
    rows, d = cond.shape
    n = w_ada.shape[1]
    bn = 1536
    return pl.pallas_call(
        _adaln_kernel,
        grid=(n // bn,),
        in_specs=[pl.BlockSpec((rows, d), lambda j: (0, 0)),
                  pl.BlockSpec((d, bn), lambda j: (0, j)),
                  pl.BlockSpec((1, bn), lambda j: (0, j))],
        out_specs=pl.BlockSpec((rows, bn), lambda j: (0, j)),
        out_shape=jax.ShapeDtypeStruct((rows, n), F32),
        compiler_params=_params(("parallel",)),
        name="adaln",
    )(cond, w_ada, b_ada.reshape(1, n))


def _inproj_kernel(x_ref, mod_ref, nw_ref, wm_ref, wgt_ref, bgt_ref, *rest, tm, fourier):
    if fourier:
        wf_ref, xm_ref, v_ref, z_ref, gt_ref, u_ref, h_scr = rest
    else:
        xm_ref, v_ref, gt_ref = rest
    mod = mod_ref[0]
    shift, scale = mod[0:1, :], mod[1:2, :]
    h = _rms(x_ref[...], nw_ref[...]) * (1.0 + scale) + shift
    hb = h.astype(BF16)
    p = _dot(hb, wm_ref[...])
    xm_ref[...] = p[:, 0:M_W].astype(BF16)
    v_ref[...] = p[:, M_W:2 * M_W].astype(BF16)
    gt_ref[0] = lax.dot_general(wgt_ref[...], hb, (((1,), (1,)), ((), ())),
                                preferred_element_type=F32) + bgt_ref[...]
    if fourier:
        z_ref[...] = p[:, 2 * M_W:3 * M_W].astype(BF16)
        nlb = h.shape[1] // 128
        for lb in range(nlb):
            h_scr[lb] = h[:, lb * 128:(lb + 1) * 128]
        sub = tm // DFT_RADIX
        hs = jnp.concatenate(
            [jnp.concatenate([h_scr[lb, pl.ds(n1, sub, stride=DFT_RADIX), :] for lb in range(nlb)], axis=1)
             for n1 in range(DFT_RADIX)], axis=0)
        uf = _dot(hs.astype(BF16), wf_ref[...])
        for n1 in range(DFT_RADIX):
            u_ref[0, n1] = uf[n1 * sub:(n1 + 1) * sub, :].astype(BF16)


def _inproj(x2d, mod, mod_row, norm_w, w_in, b_gate, seq_len, tm, fourier):
    n, d = x2d.shape
    tiles_per_seq = seq_len // tm
    nb = n // seq_len
    w_in_b = w_in.astype(BF16)
    w_xm_v = w_in_b[:, F_W:F_W + 2 * M_W]
    w_gt = w_in_b[:, F_W + 2 * M_W:MLSTM_END].T
    if fourier:
        w_main = jnp.concatenate([w_xm_v, w_in_b[:, MLSTM_END:]], axis=1)
    else:
        w_main = w_xm_v
    nm = w_main.shape[1]
    tok = lambda i: (i, 0)
    const = lambda i: (0, 0)
    in_specs = [pl.BlockSpec((tm, d), tok),
                pl.BlockSpec((1, 6, d), lambda i: (mod_row(i), 0, 0)),
                pl.BlockSpec((1, d), const),
                pl.BlockSpec((d, nm), const),
                pl.BlockSpec((N_GATES, d), const),
                pl.BlockSpec((N_GATES, 1), const)]
    args = [x2d, mod, norm_w.reshape(1, d), w_main, w_gt, b_gate.reshape(N_GATES, 1)]
    tokb = pl.BlockSpec((tm, M_W), tok)
    out_specs = [tokb, tokb]
    out_shape = [jax.ShapeDtypeStruct((n, M_W), BF16)] * 2
    if fourier:
        in_specs.append(pl.BlockSpec((d, F_W), const))
        args.append(w_in_b[:, :F_W])
        out_specs.append(tokb)
        out_shape.append(jax.ShapeDtypeStruct((n, M_W), BF16))
    out_specs.append(pl.BlockSpec((1, N_GATES, tm), lambda i: (i // tiles_per_seq, 0, i % tiles_per_seq)))
    out_shape.append(jax.ShapeDtypeStruct((nb, N_GATES, seq_len), F32))
    scratch = []
    if fourier:
        sub = tm // DFT_RADIX
        out_specs.append(pl.BlockSpec((1, DFT_RADIX, sub, F_W),
                                      lambda i: (i // tiles_per_seq, 0, i % tiles_per_seq, 0)))
        out_shape.append(jax.ShapeDtypeStruct((nb, DFT_RADIX, seq_len // DFT_RADIX, F_W), BF16))
        scratch.append(pltpu.VMEM((d // 128, tm, 128), F32))
    return pl.pallas_call(
        functools.partial(_inproj_kernel, tm=tm, fourier=fourier),
        grid=(n // tm,),
        in_specs=in_specs, out_specs=out_specs, out_shape=out_shape,
        scratch_shapes=scratch,
        compiler_params=_params(("parallel",)),
        name="inproj_latent" if fourier else "inproj_ctx",
    )(*args)


def _dft_tables(seq_len):
    sub = seq_len // DFT_RADIX
    k2 = np.arange(sub, dtype=np.int64)[None, :, None]
    n1 = np.arange(DFT_RADIX, dtype=np.int64)[:, None, None]
    n2 = np.arange(sub, dtype=np.int64)[None, None, :]
    ang = 2.0 * np.pi * ((k2 * (n1 + DFT_RADIX * n2)) % seq_len).astype(np.float64) / seq_len
    scale = 1.0 / math.sqrt(seq_len)
    return (jnp.asarray(np.cos(ang) * scale, dtype=F32).astype(BF16),
            jnp.asarray(np.sin(ang) * scale, dtype=F32).astype(BF16))


def _cadd(a, b):
    return (a[0] + b[0], a[1] + b[1])


def _csub(a, b):
    return (a[0] - b[0], a[1] - b[1])


def _cmul_w8(z, k):
    re, im = z
    r = math.sqrt(0.5)
    if k == 0:
        return z
    if k == 1:
        return ((re + im) * r, (im - re) * r)
    if k == 2:
        return (im, -re)
    return ((im - re) * r, (-re - im) * r)


def _dft8(z):
    def dft4(a, b, c, d):
        s0, s1 = _cadd(a, c), _csub(a, c)
        t0, t1 = _cadd(b, d), _csub(b, d)
        t1r = _cmul_w8(t1, 2)
        return [_cadd(s0, t0), _cadd(s1, t1r), _csub(s0, t0), _csub(s1, t1r)]
    ev = dft4(z[0], z[2], z[4], z[6])
    od = dft4(z[1], z[3], z[5], z[7])
    out = [None] * 8
    for k in range(4):
        t = _cmul_w8(od[k], k)
        out[k] = _cadd(ev[k], t)
        out[k + 4] = _csub(ev[k], t)
    return out


def _fourier_kernel(u_ref, tc_ref, ts_ref, p_ref, q_ref, ga_ref, gb_ref, *, sub, rb):
    for n1 in range(DFT_RADIX):
        un = u_ref[0, n1]
        ga_ref[n1] = _dot(tc_ref[n1], un)
        gb_ref[n1] = _dot(ts_ref[n1], un)

    def body(i, carry):
        r0 = pl.multiple_of(i * rb, rb)
        z = [(ga_ref[n1, pl.ds(r0, rb), :], -gb_ref[n1, pl.ds(r0, rb), :]) for n1 in range(DFT_RADIX)]
        x = _dft8(z)
        for k1 in range(DFT_RADIX):
            p_ref[pl.ds(k1 * sub + r0, rb), :] = x[k1][0].astype(BF16)
            q_ref[pl.ds(k1 * sub + r0, rb), :] = (-x[k1][1]).astype(BF16)
        return carry

    lax.fori_loop(0, sub // rb, body, 0)


def _fourier(u_perm, seq_len):
    nb = u_perm.shape[0]
    sub = seq_len // DFT_RADIX
    cb = 256
    tc, ts = _dft_tables(seq_len)
    return pl.pallas_call(
        functools.partial(_fourier_kernel, sub=sub, rb=32),
        grid=(nb, F_W // cb),
        in_specs=[pl.BlockSpec((1, DFT_RADIX, sub, cb), lambda b, j: (b, 0, 0, j)),
                  pl.BlockSpec((DFT_RADIX, sub, sub), lambda b, j: (0, 0, 0)),
                  pl.BlockSpec((DFT_RADIX, sub, sub), lambda b, j: (0, 0, 0))],
        out_specs=[pl.BlockSpec((seq_len, cb), lambda b, j: (b, j))] * 2,
        out_shape=[jax.ShapeDtypeStruct((nb * seq_len, F_W), BF16)] * 2,
        scratch_shapes=[pltpu.VMEM((DFT_RADIX, sub, cb), F32)] * 2,
        compiler_params=_params(("parallel", "parallel")),
        name="fourier",
    )(u_perm, tc, ts)


def _prep_kernel(xm_ref, prev_ref, next_ref, cw_ref, cb_ref, wq_ref, wkt_ref, a_ref, q_ref, kt_ref,
                 *, t, tiles_per_seq):
    i = pl.program_id(0)
    first = (i % tiles_per_seq) == 0
    last = (i % tiles_per_seq) == tiles_per_seq - 1
    x = xm_ref[...].astype(F32)
    prev_row = jnp.where(first, 0.0, prev_ref[15:16, :].astype(F32))
    next_row = jnp.where(last, 0.0, next_ref[0:1, :].astype(F32))
    rows = lax.broadcasted_iota(jnp.int32, (t, 1), 0)
    xp = jnp.where(rows == 0, prev_row, pltpu.roll(x, 1, 0))
    xn = jnp.where(rows == t - 1, next_row, pltpu.roll(x, t - 1, 0))
    cw = cw_ref[...]
    a = _silu(cw[0:1, :] * xp + cw[1:2, :] * x + cw[2:3, :] * xn + cb_ref[...])
    ab = a.astype(BF16)
    a_ref[...] = ab
    for pair in range(2):
        sl = slice(pair * 256, (pair + 1) * 256)
        q_ref[:, sl] = _dot(ab[:, sl], wq_ref[pair]).astype(BF16)
        kt = lax.dot_general(wkt_ref[pair], ab[:, sl], (((1,), (1,)), ((), ())),
                             preferred_element_type=F32)
        kt_ref[0, sl, :] = (kt * M_HEAD_DIM ** -0.5).astype(BF16)


def _pair_blockdiag(w):
    z = jnp.zeros_like(w[0])
    return jnp.stack([jnp.block([[w[0], z], [z, w[1]]]), jnp.block([[w[2], z], [z, w[3]]])])


def _prep(xm, mconv_w, mconv_b, w_q, w_k, seq_len, t):
    n = xm.shape[0]
    nb = n // seq_len
    tiles_per_seq = seq_len // t
    hb = t // 16
    nhalo = n // 16
    wq_bd = _pair_blockdiag(w_q.astype(BF16))
    wkt_bd = _pair_blockdiag(jnp.swapaxes(w_k, 1, 2).astype(BF16))
    return pl.pallas_call(
        functools.partial(_prep_kernel, t=t, tiles_per_seq=tiles_per_seq),
        grid=(n // t,),
        in_specs=[pl.BlockSpec((t, M_W), lambda i: (i, 0)),
                  pl.BlockSpec((16, M_W), lambda i: (jnp.maximum(i * hb - 1, 0), 0)),
                  pl.BlockSpec((16, M_W), lambda i: (jnp.minimum((i + 1) * hb, nhalo - 1), 0)),
                  pl.BlockSpec((3, M_W), lambda i: (0, 0)),
                  pl.BlockSpec((1, M_W), lambda i: (0, 0)),
                  pl.BlockSpec((2, 256, 256), lambda i: (0, 0, 0)),
                  pl.BlockSpec((2, 256, 256), lambda i: (0, 0, 0))],
        out_specs=[pl.BlockSpec((t, M_W), lambda i: (i, 0)),
                   pl.BlockSpec((t, M_W), lambda i: (i, 0)),
                   pl.BlockSpec((1, M_W, t), lambda i: (i // tiles_per_seq, 0, i % tiles_per_seq))],
        out_shape=[jax.ShapeDtypeStruct((n, M_W), BF16),
                   jax.ShapeDtypeStruct((n, M_W), BF16),
                   jax.ShapeDtypeStruct((nb, M_W, seq_len), BF16)],
        compiler_params=_params(("parallel",)),
        name="prep_%d" % seq_len,
    )(xm, xm, xm, mconv_w, mconv_b.reshape(1, M_W), wq_bd, wkt_bd)


def _log_sigmoid(x):
    return jnp.minimum(x, 0.0) - jnp.log1p(jnp.exp(-jnp.abs(x)))


def _tri(reverse):
    r = lax.broadcasted_iota(jnp.int32, (CHUNK, CHUNK), 0)
    c = lax.broadcasted_iota(jnp.int32, (CHUNK, CHUNK), 1)
    return (c >= r) if reverse else (c <= r)


def _cumsum_rows(gt, reverse):
    t_row = _tri(not reverse).astype(BF16)
    lf = _log_sigmoid(gt)
    l0, l1, l2 = _split3(lf)
    return lf, _dot(l0, t_row) + _dot(l1, t_row) + _dot(l2, t_row)


def _twice(x):
    return jnp.concatenate([x, x], axis=-1)


def _pair_step(q2, kts, vaugs, causal, lf_rows, brows, li_rows, c_augs, ms, with_output):
    kws, m_news, w_olds, scales, m_ts, w_inters = [], [], [], [], [], []
    for e in range(2):
        rowterm = li_rows[e] - brows[e]
        b_end = jnp.sum(lf_rows[e], axis=-1, keepdims=True)
        g = b_end + rowterm
        m_new = jnp.maximum(b_end + ms[e], jnp.max(g, axis=-1, keepdims=True))
        w_olds.append(jnp.exp(b_end + ms[e] - m_new))
        kws.append((kts[e].astype(F32) * jnp.exp(g - m_new)).astype(BF16))
        m_news.append(m_new)
        if with_output:
            bcol = jnp.sum(jnp.where(causal, lf_rows[e], 0.0), axis=-1, keepdims=True)
            dmat = jnp.where(causal, bcol + rowterm, -jnp.inf)
            inter = bcol + ms[e]
            m_t = jnp.maximum(inter, jnp.max(dmat, axis=-1, keepdims=True))
            w_inters.append(jnp.exp(inter - m_t))
            scales.append(jnp.exp(dmat - m_t))
            m_ts.append(m_t)
    zk = jnp.zeros((CHUNK, CHUNK), BF16)
    vv = jnp.concatenate(vaugs, axis=0)
    if not with_output:
        sv = _dot(jnp.concatenate([jnp.concatenate([kws[0], zk], axis=-1),
                                   jnp.concatenate([zk, kws[1]], axis=-1)], axis=0), vv)
        return [None, None], [_twice(w_olds[e]) * c_augs[e] + sv[e * CHUNK:(e + 1) * CHUNK] for e in range(2)], m_news
    zc = jnp.zeros((M_HEAD_DIM, 3 * M_HEAD_DIM), BF16)
    rhs = jnp.concatenate(
        [jnp.concatenate([kts[0], c_augs[0].astype(BF16), zc], axis=-1),
         jnp.concatenate([zc, kts[1], c_augs[1].astype(BF16)], axis=-1)], axis=0)
    qkc = _dot(q2, rhs)
    lhs = []
    for e in range(2):
        s = (qkc[:, 3 * CHUNK * e:3 * CHUNK * e + CHUNK] * scales[e]).astype(BF16)
        lhs += [jnp.concatenate([s, zk] if e == 0 else [zk, s], axis=-1),
                jnp.concatenate([kws[e], zk] if e == 0 else [zk, kws[e]], axis=-1)]
    sv = _dot(jnp.concatenate(lhs, axis=0), vv)
    hs, c_news = [], []
    for e in range(2):
        nd = _twice(w_inters[e]) * qkc[:, 3 * CHUNK * e + CHUNK:3 * CHUNK * (e + 1)] + sv[2 * e * CHUNK:(2 * e + 1) * CHUNK]
        hs.append(nd[:, :M_HEAD_DIM] / jnp.maximum(jnp.abs(nd[:, M_HEAD_DIM:]), jnp.exp(-m_ts[e])))
        c_news.append(_twice(w_olds[e]) * c_augs[e] + sv[(2 * e + 1) * CHUNK:(2 * e + 2) * CHUNK])
    return hs, c_news, m_news


def _scan_kernel(qf_ref, ktf_ref, vf_ref, gtf_ref, qb_ref, ktb_ref, vb_ref, gtb_ref,
                 ktc_ref, vc_ref, gtc_ref, hf_ref, hb_ref, c_scr, m_scr, *, ctx_chunks, nbb):
    c = pl.program_id(1)
    ones = jnp.ones((CHUNK, M_HEAD_DIM), BF16)

    def run_chunk(bb, q, kt, v, gt, d, with_output):
        reverse = d == 1
        causal = _tri(reverse)
        lf, brow = _cumsum_rows(gt, reverse)
        outs = []
        for pair in range(M_HEADS // 2):
            heads = (2 * pair, 2 * pair + 1)
            sls = [slice(hd * M_HEAD_DIM, (hd + 1) * M_HEAD_DIM) for hd in heads]
            gis = [d * 8 + hd for hd in heads]
            gfs = [d * 8 + 4 + hd for hd in heads]
            rs = [(bb * 2 + d) * M_HEADS + hd for hd in heads]
            hs, c_news, m_news = _pair_step(
                q[:, sls[0].start:sls[1].stop] if with_output else None,
                [kt[sl, :] for sl in sls], [jnp.concatenate([v[:, sl], ones], axis=-1) for sl in sls], causal,
                [lf[gf:gf + 1, :] for gf in gfs], [brow[gf:gf + 1, :] for gf in gfs],
                [gt[gi:gi + 1, :] for gi in gis], [c_scr[r] for r in rs], [m_scr[r, 0:1, :] for r in rs],
                with_output)
            for e, r in enumerate(rs):
                c_scr[r] = c_news[e]
                m_scr[r] = jnp.broadcast_to(m_news[e], (8, 128))
            outs += hs
        return outs

    @pl.when(c == 0)
    def _():
        c_scr[...] = jnp.zeros_like(c_scr)
        m_scr[...] = jnp.zeros_like(m_scr)
        for bb in range(nbb):
            for d in range(2):
                order = range(ctx_chunks) if d == 0 else range(ctx_chunks - 1, -1, -1)
                for j in order:
                    rs = slice(j * CHUNK, (j + 1) * CHUNK)
                    run_chunk(bb, None, ktc_ref[bb, :, rs], vc_ref[bb, rs, :], gtc_ref[bb, :, rs], d, False)

    for bb in range(nbb):
        hf = run_chunk(bb, qf_ref[bb], ktf_ref[bb], vf_ref[bb], gtf_ref[bb], 0, True)
        hb = run_chunk(bb, qb_ref[bb], ktb_ref[bb], vb_ref[bb], gtb_ref[bb], 1, True)
        for hd in range(M_HEADS):
            sl = slice(hd * M_HEAD_DIM, (hd + 1) * M_HEAD_DIM)
            hf_ref[bb, :, sl] = hf[hd].astype(BF16)
            hb_ref[bb, :, sl] = hb[hd].astype(BF16)


def _scan(q, kt, v, gt, kt_c, v_c, gt_c, seq_len, ctx_len, nbb):
    n = q.shape[0]
    nb = n // seq_len
    nc = seq_len // CHUNK
    q3, v3 = q.reshape(nb, seq_len, M_W), v.reshape(nb, seq_len, M_W)
    fwd = lambda b, c: (b, c, 0)
    bwd = lambda b, c: (b, nc - 1 - c, 0)
    fwd_t = lambda b, c: (b, 0, c)
    bwd_t = lambda b, c: (b, 0, nc - 1 - c)
    tokb = lambda im: pl.BlockSpec((nbb, CHUNK, M_W), im)

    def side(tok_map, tok_map_t):
        return [tokb(tok_map), pl.BlockSpec((nbb, M_W, CHUNK), tok_map_t), tokb(tok_map),
                pl.BlockSpec((nbb, N_GATES, CHUNK), tok_map_t)]

    in_specs = (side(fwd, fwd_t) + side(bwd, bwd_t)
                + [pl.BlockSpec((nbb, M_W, ctx_len), lambda b, c: (b, 0, 0)),
                   pl.BlockSpec((nbb, ctx_len, M_W), lambda b, c: (b, 0, 0)),
                   pl.BlockSpec((nbb, N_GATES, ctx_len), lambda b, c: (b, 0, 0))])
    hf, hb = pl.pallas_call(
        functools.partial(_scan_kernel, ctx_chunks=ctx_len // CHUNK, nbb=nbb),
        grid=(nb // nbb, nc),
        in_specs=in_specs,
        out_specs=[tokb(fwd), tokb(bwd)],
        out_shape=[jax.ShapeDtypeStruct((nb, seq_len, M_W), BF16)] * 2,
        scratch_shapes=[pltpu.VMEM((nbb * 2 * M_HEADS, M_HEAD_DIM, 2 * M_HEAD_DIM), F32),
                        pltpu.VMEM((nbb * 2 * M_HEADS, 8, 128), F32)],
        compiler_params=_params(("parallel", "arbitrary")),
        name="mlstm_scan",
    )(q3, kt, v3, gt, q3, kt, v3, gt, kt_c, v_c.reshape(nb, ctx_len, M_W), gt_c)
    return hf.reshape(n, M_W), hb.reshape(n, M_W)


def _mixout_kernel(x_ref, mod_ref, p_ref, q_ref, hf_ref, hb_ref, a_ref, z_ref, mnw_ref, msk_ref,
                   wc_ref, wo_ref, n2w_ref, x1_ref, h2_ref):
    mod = mod_ref[0]
    gate1, shift2, scale2 = mod[2:3, :], mod[3:4, :], mod[4:5, :]
    yf = _dot(p_ref[...], wc_ref[0]) + _dot(q_ref[...], wc_ref[1])
    h = hf_ref[...].astype(F32) + hb_ref[...].astype(F32)
    parts = []
    for hd in range(M_HEADS):
        hh = h[:, hd * M_HEAD_DIM:(hd + 1) * M_HEAD_DIM]
        mu = jnp.mean(hh, axis=-1, keepdims=True)
        dlt = hh - mu
        var = jnp.mean(dlt * dlt, axis=-1, keepdims=True)
        parts.append(dlt * lax.rsqrt(var + NORM_EPS))
    hn = jnp.concatenate(parts, axis=-1) * mnw_ref[...]
    ym = (hn + msk_ref[...] * a_ref[...].astype(F32)) * _silu(z_ref[...].astype(F32))
    y = _dot(yf.astype(BF16), wo_ref[0]) + _dot(ym.astype(BF16), wo_ref[1])
    x1 = x_ref[...] + gate1 * y
    x1_ref[...] = x1
    h2_ref[...] = (_rms(x1, n2w_ref[...]) * (1.0 + scale2) + shift2).astype(BF16)


def _channel_dft():
    k = np.arange(F_GROUP_W, dtype=np.int64)
    ang = 2.0 * np.pi * ((k[:, None] * k[None, :]) % F_GROUP_W).astype(np.float64) / F_GROUP_W
    scale = 1.0 / math.sqrt(F_GROUP_W)
    eye = np.eye(F_GROUPS)
    wc = np.stack([np.kron(eye, np.cos(ang) * scale), -np.kron(eye, np.sin(ang) * scale)])
    return jnp.asarray(wc, dtype=F32).astype(BF16)


def _mixout(x2d, mod, p, q, hf, hb, a, z, mnorm_w, m_skip, w_out, norm2_w, seq_len, tm):
    n, d = x2d.shape
    tiles_per_seq = seq_len // tm
    tok = lambda i: (i, 0)
    half = pl.BlockSpec((tm, F_W), tok)
    full = pl.BlockSpec((tm, d), tok)
    vec = lambda w: pl.BlockSpec((1, w), lambda i: (0, 0))
    return pl.pallas_call(
        _mixout_kernel,
        grid=(n // tm,),
        in_specs=[full, pl.BlockSpec((1, 6, d), lambda i: (i // tiles_per_seq, 0, 0)),
                  half, half, half, half, half, half, vec(M_W), vec(M_W),
                  pl.BlockSpec((2, F_W, F_W), lambda i: (0, 0, 0)),
                  pl.BlockSpec((2, F_W, d), lambda i: (0, 0, 0)),
                  vec(d)],
        out_specs=[full, full],
        out_shape=[jax.ShapeDtypeStruct((n, d), F32), jax.ShapeDtypeStruct((n, d), BF16)],
        compiler_params=_params(("parallel",)),
        name="mixout",
    )(x2d, mod, p, q, hf, hb, a, z, mnorm_w.reshape(1, M_W), m_skip.reshape(1, M_W),
      _channel_dft(), w_out.astype(BF16).reshape(2, F_W, d), norm2_w.reshape(1, d))


def _zero_of(x):
    bits = pltpu.bitcast(x, jnp.uint32)
    return pltpu.bitcast(lax.shift_right_logical(lax.shift_right_logical(bits, jnp.uint32(16)), jnp.uint32(16)), F32)


def _ffn_kernel(h_ref, hp_ref, hn_ref, x1_ref, mod_ref, wu_ref, bu_ref, cw_ref, cb_ref, wd_ref, bd_ref,
                fw_ref, o_ref, hcat_scr, u0_scr, u1_scr, act0_scr, act1_scr, *, rows, tiles, cw):
    t = pl.program_id(1)
    nt = rows * GRID_W
    ne = nt + 2 * GRID_W
    nj = D_FF // cw
    n_uc = ne // FFN_UP_ROWS
    n_dc = nt // FFN_DOWN_ROWS
    hcat_scr[0:GRID_W] = hp_ref[...]
    hcat_scr[GRID_W:GRID_W + nt] = h_ref[...]
    hcat_scr[GRID_W + nt:] = hn_ref[...]
    row8 = lax.broadcasted_iota(jnp.int32, (8, 1), 0)
    o_ref[...] = jnp.zeros_like(o_ref)

    def up_chunk(j, u_scr, c, after=None):
        r0 = c * FFN_UP_ROWS
        lhs = hcat_scr[r0:r0 + FFN_UP_ROWS, :]
        if after is not None:
            lhs = lhs + jnp.concatenate([_zero_of(after)] * (lhs.shape[1] // 128), axis=-1).astype(BF16)
        u = _dot(lhs, wu_ref[j]) + bu_ref[j]
        if c == 0:
            u_scr[0:GRID_W] = jnp.where(t > 0, u[0:GRID_W], 0.0)
            u_scr[GRID_W:FFN_UP_ROWS] = u[GRID_W:]
        elif c == n_uc - 1:
            u_scr[r0:ne - GRID_W] = u[:FFN_UP_ROWS - GRID_W]
            u_scr[ne - GRID_W:] = jnp.where(t < tiles - 1, u[FFN_UP_ROWS - GRID_W:], 0.0)
        else:
            u_scr[r0:r0 + FFN_UP_ROWS] = u

    def conv_row(j, u_scr, a_scr, r):
        w = cw_ref[j]
        cb = cb_ref[j]
        rows3 = [slice((r + dr) * GRID_W, (r + dr + 1) * GRID_W) for dr in range(3)]

        def conv_lanes(ls):
            pcs = [u_scr[rs, ls] for rs in rows3]
            left = w[0:1, ls] * pcs[0] + w[3:4, ls] * pcs[1] + w[6:7, ls] * pcs[2]
            mid = w[1:2, ls] * pcs[0] + w[4:5, ls] * pcs[1] + w[7:8, ls] * pcs[2]
            right = w[2:3, ls] * pcs[0] + w[5:6, ls] * pcs[1] + w[8:9, ls] * pcs[2]
            rl = pltpu.roll(left, 1, 0)
            rr = pltpu.roll(right, GRID_W - 1, 0)
            rl = jnp.concatenate([jnp.where(row8 == 0, 0.0, rl[0:8]), rl[8:]], axis=0)
            rr = jnp.concatenate([rr[:GRID_W - 8], jnp.where(row8 == 7, 0.0, rr[GRID_W - 8:])], axis=0)
            return cb[:, ls] + mid + rl + rr

        tag = None
        for k in range(cw // 128):
            val = conv_lanes(slice(k * 128, (k + 1) * 128))
            gate = conv_lanes(slice(cw + k * 128, cw + (k + 1) * 128))
            a_scr[r * GRID_W:(r + 1) * GRID_W, k * 128:(k + 1) * 128] = (val * _silu(gate)).astype(BF16)
            tag = gate[0:1, :]
        return tag

    def down_chunk(j, a_scr, c, after=None):
        rs = slice(c * FFN_DOWN_ROWS, (c + 1) * FFN_DOWN_ROWS)
        lhs = a_scr[rs, :]
        if after is not None:
            lhs = lhs + jnp.concatenate([_zero_of(after)] * (cw // 128), axis=-1).astype(BF16)
        o_ref[rs, :] += _dot(lhs, wd_ref[j])

    def stage(up=None, conv=None, down=None):
        mm = []
        if up is not None:
            mm += [functools.partial(up_chunk, *up, c) for c in range(n_uc)]
        if down is not None:
            mm += [functools.partial(down_chunk, *down, c) for c in range(n_dc)]
        if conv is None:
            for fn in mm:
                fn(after=None)
            return
        gate = {}
        for k in range(1, len(mm)):
            gate.setdefault(min((k * rows) // len(mm), rows - 1), []).append(k)
        if mm:
            mm[0](after=None)
        for r in range(rows):
            tag = conv_row(*conv, r)
            for k in gate.get(r, []):
                mm[k](after=tag)

    stage(up=(0, u0_scr))
    stage(up=(1, u1_scr), conv=(0, u0_scr, act0_scr))

    def body(jj, carry):
        j = 2 * jj + 1
        stage(up=(j + 1, u0_scr), conv=(j, u1_scr, act1_scr), down=(j - 1, act0_scr))
        stage(up=(j + 2, u1_scr), conv=(j + 1, u0_scr, act0_scr), down=(j, act1_scr))
        return carry

    lax.fori_loop(0, (nj - 2) // 2, body, 0)
    stage(conv=(nj - 1, u1_scr, act1_scr), down=(nj - 2, act0_scr))
    stage(down=(nj - 1, act1_scr))
    mod = mod_ref[0]
    y = o_ref[...] + bd_ref[...]
    o_ref[...] = _rms(x1_ref[...] + mod[5:6, :] * y, fw_ref[...])


def _ffn(h2, x1, mod, w_up, b_up, fconv_w, fconv_b, w_down, b_down, final_norm_w, seq_len, rows, cw):
    n, d = x1.shape
    nb = n // seq_len
    grid_h = seq_len // GRID_W
    tiles = grid_h // rows
    nt = rows * GRID_W
    nj = D_FF // cw

    def pair(w):
        lead = w.shape[:-1]
        w2 = w.reshape(lead + (2, nj, cw))
        w2 = jnp.moveaxis(w2, -2, 0)
        return w2.reshape((nj,) + lead + (2 * cw,))

    wu = pair(w_up.astype(BF16))
    bu = pair(b_up.reshape(1, 2 * D_FF))
    cwt = pair(fconv_w.reshape(9, 2 * D_FF))
    cbt = pair(fconv_b.reshape(1, 2 * D_FF))
    wd = w_down.astype(BF16).reshape(nj, cw, d)
    nblk = n // GRID_W
    const3 = lambda b, t: (0, 0, 0)
    return pl.pallas_call(
        functools.partial(_ffn_kernel, rows=rows, tiles=tiles, cw=cw),
        grid=(nb, tiles),
        in_specs=[pl.BlockSpec((nt, d), lambda b, t: (b * tiles + t, 0)),
                  pl.BlockSpec((GRID_W, d), lambda b, t: (jnp.maximum((b * tiles + t) * rows - 1, 0), 0)),
                  pl.BlockSpec((GRID_W, d), lambda b, t: (jnp.minimum((b * tiles + t + 1) * rows, nblk - 1), 0)),
                  pl.BlockSpec((nt, d), lambda b, t: (b * tiles + t, 0)),
                  pl.BlockSpec((1, 6, d), lambda b, t: (b, 0, 0)),
                  pl.BlockSpec((nj, d, 2 * cw), const3, pipeline_mode=pl.Buffered(1)),
                  pl.BlockSpec((nj, 1, 2 * cw), const3),
                  pl.BlockSpec((nj, 9, 2 * cw), const3),
                  pl.BlockSpec((nj, 1, 2 * cw), const3),
                  pl.BlockSpec((nj, cw, d), const3, pipeline_mode=pl.Buffered(1)),
                  pl.BlockSpec((1, d), lambda b, t: (0, 0)),
                  pl.BlockSpec((1, d), lambda b, t: (0, 0))],
        out_specs=pl.BlockSpec((nt, d), lambda b, t: (b * tiles + t, 0)),
        out_shape=jax.ShapeDtypeStruct((n, d), F32),
        scratch_shapes=[pltpu.VMEM((nt + 2 * GRID_W, d), BF16),
                        pltpu.VMEM((nt + 2 * GRID_W, 2 * cw), F32),
                        pltpu.VMEM((nt + 2 * GRID_W, 2 * cw), F32),
                        pltpu.VMEM((nt, cw), BF16),
                        pltpu.VMEM((nt, cw), BF16)],
        compiler_params=_params(("parallel", "parallel")),
        name="conv_ffn",
    )(h2, h2, h2, x1, mod, wu, bu, cwt, cbt, wd, b_down.reshape(1, d), final_norm_w.reshape(1, d))


def kernel(x, c, ctx, c_ctx, w_ada, b_ada, norm1_w, w_in, mconv_w, mconv_b, w_q, w_k, b_gate, mnorm_w, m_skip,
           w_out, norm2_w, w_up, b_up, fconv_w, fconv_b, w_down, b_down, final_norm_w):
    bsz, seq_len, d = x.shape
    ctx_len = ctx.shape[1]
    assert w_ada.shape[0] == 1, "single-layer kernel"
    cond = jnp.concatenate([c, c_ctx[None, :], jnp.zeros((16 - bsz - 1, d), F32)], axis=0)
    mod = _adaln(cond, w_ada[0], b_ada[0]).reshape(16, 6, d)

    x2d = x.reshape(bsz * seq_len, d)
    ctx2d = ctx.reshape(bsz * ctx_len, d)

    xm_c, v_c, gt_c = _inproj(ctx2d, mod, lambda i: bsz, norm1_w[0], w_in[0], b_gate[0],
                              ctx_len, ctx_len, False)
    _, _, kt_c = _prep(xm_c, mconv_w[0], mconv_b[0], w_q[0], w_k[0], ctx_len, ctx_len)

    tm = 512
    xm, v, z, gt, u_perm = _inproj(x2d, mod, lambda i: i // (seq_len // tm), norm1_w[0], w_in[0],
                                   b_gate[0], seq_len, tm, True)
    p, q_im = _fourier(u_perm, seq_len)
    a, q, kt = _prep(xm, mconv_w[0], mconv_b[0], w_q[0], w_k[0], seq_len, 512)
    hf, hb = _scan(q, kt, v, gt, kt_c, v_c, gt_c, seq_len, ctx_len, SCAN_SEQS)
    x1, h2 = _mixout(x2d, mod, p, q_im, hf, hb, a, z, mnorm_w[0], m_skip[0], w_out[0], norm2_w[0],
                     seq_len, tm)
    out = _ffn(h2, x1, mod, w_up[0], b_up[0], fconv_w[0], fconv_b[0], w_down[0], b_down[0],
               final_norm_w, seq_len, FFN_GRID_ROWS, FFN_CHANNELS)
    return out.reshape(bsz, seq_len, d)
```

```python
import functools
import math

import numpy as np
import jax
import jax.numpy as jnp
from jax import lax
from jax.experimental import pallas as pl
from jax.experimental.pallas import tpu as pltpu

F32 = jnp.float32
BF16 = jnp.bfloat16

D_MODEL = 1024
GRID_W = 64
F_W = 512
F_GROUPS = 4
F_GROUP_W = F_W // F_GROUPS
M_W = 512
M_HEADS = 4
M_HEAD_DIM = M_W // M_HEADS
N_GATES = 16
MLSTM_END = F_W + 2 * M_W + N_GATES
CHUNK = 128
D_FF = 5 * D_MODEL // 2
NORM_EPS = 1e-6

DFT_RADIX = 8
VMEM_LIMIT = 56 * 1024 * 1024
SCAN_SEQS = 4
FFN_GRID_ROWS = 16
FFN_CHANNELS = 256
CONV_DTYPE = BF16


def _params(sem, vmem=VMEM_LIMIT):
    return pltpu.CompilerParams(dimension_semantics=sem, vmem_limit_bytes=vmem)


def _split3(a):
    hi = a.astype(BF16)
    r1 = a - hi.astype(F32)
    mid = r1.astype(BF16)
    lo = (r1 - mid.astype(F32)).astype(BF16)
    return hi, mid, lo


def _dot(a, b):
    return jnp.dot(a, b, preferred_element_type=F32)


def _dot_f32(a, b):
    a0, a1, a2 = _split3(a)
    b0, b1, b2 = _split3(b)
    return (_dot(a0, b0) + (_dot(a0, b1) + _dot(a1, b0))
            + (_dot(a1, b1) + _dot(a0, b2) + _dot(a2, b0)))


def _silu(x):
    return x * jax.nn.sigmoid(x)


def _rms(x, w):
    return x * lax.rsqrt(jnp.mean(x * x, axis=-1, keepdims=True) + NORM_EPS) * w


def _adaln_kernel(cond_ref, w_ref, b_ref, o_ref):
    s = _silu(cond_ref[...])
    o_ref[...] = _dot_f32(s, w_ref[...]) + b_ref[...]


def _adaln(cond, w_ada, b_ada):
    rows, d = cond.shape
    n = w_ada.shape[1]
    bn = 1536
    return pl.pallas_call(
        _adaln_kernel,
        grid=(n // bn,),
        in_specs=[pl.BlockSpec((rows, d), lambda j: (0, 0)),
                  pl.BlockSpec((d, bn), lambda j: (0, j)),
                  pl.BlockSpec((1, bn), lambda j: (0, j))],
        out_specs=pl.BlockSpec((rows, bn), lambda j: (0, j)),
        out_shape=jax.ShapeDtypeStruct((rows, n), F32),
        compiler_params=_params(("parallel",)),
        name="adaln",
    )(cond, w_ada, b_ada.reshape(1, n))


def _inproj_kernel(x_ref, mod_ref, nw_ref, wm_ref, wgt_ref, bgt_ref, *rest, tm, fourier):
    if fourier:
        wf_ref, xm_ref, v_ref, z_ref, gt_ref, u_ref, h_scr = rest
    else:
        xm_ref, v_ref, gt_ref = rest
    mod = mod_ref[0]
    shift, scale = mod[0:1, :], mod[1:2, :]
    h = _rms(x_ref[...], nw_ref[...]) * (1.0 + scale) + shift
    hb = h.astype(BF16)
    p = _dot(hb, wm_ref[...])
    xm_ref[...] = p[:, 0:M_W].astype(BF16)
    v_ref[...] = p[:, M_W:2 * M_W].astype(BF16)
    gt_ref[0] = lax.dot_general(wgt_ref[...], hb, (((1,), (1,)), ((), ())),
                                preferred_element_type=F32) + bgt_ref[...]
    if fourier:
        z_ref[...] = p[:, 2 * M_W:3 * M_W].astype(BF16)
        nlb = h.shape[1] // 128
        for lb in range(nlb):
            h_scr[lb] = h[:, lb * 128:(lb + 1) * 128]
        sub = tm // DFT_RADIX
        hs = jnp.concatenate(
            [jnp.concatenate([h_scr[lb, pl.ds(n1, sub, stride=DFT_RADIX), :] for lb in range(nlb)], axis=1)
             for n1 in range(DFT_RADIX)], axis=0)
        uf = _dot(hs.astype(BF16), wf_ref[...])
        for n1 in range(DFT_RADIX):
            u_ref[0, n1] = uf[n1 * sub:(n1 + 1) * sub, :].astype(BF16)


def _inproj(x2d, mod, mod_row, norm_w, w_in, b_gate, seq_len, tm, fourier):
    n, d = x2d.shape
    tiles_per_seq = seq_len // tm
    nb = n // seq_len
    w_in_b = w_in.astype(BF16)
    w_xm_v = w_in_b[:, F_W:F_W + 2 * M_W]
    w_gt = w_in_b[:, F_W + 2 * M_W:MLSTM_END].T
    if fourier:
        w_main = jnp.concatenate([w_xm_v, w_in_b[:, MLSTM_END:]], axis=1)
    else:
        w_main = w_xm_v
    nm = w_main.shape[1]
    tok = lambda i: (i, 0)
    const = lambda i: (0, 0)
    in_specs = [pl.BlockSpec((tm, d), tok),
                pl.BlockSpec((1, 6, d), lambda i: (mod_row(i), 0, 0)),
                pl.BlockSpec((1, d), const),
                pl.BlockSpec((d, nm), const),
                pl.BlockSpec((N_GATES, d), const),
                pl.BlockSpec((N_GATES, 1), const)]
    args = [x2d, mod, norm_w.reshape(1, d), w_main, w_gt, b_gate.reshape(N_GATES, 1)]
    tokb = pl.BlockSpec((tm, M_W), tok)
    out_specs = [tokb, tokb]
    out_shape = [jax.ShapeDtypeStruct((n, M_W), BF16)] * 2
    if fourier:
        in_specs.append(pl.BlockSpec((d, F_W), const))
        args.append(w_in_b[:, :F_W])
        out_specs.append(tokb)
        out_shape.append(jax.ShapeDtypeStruct((n, M_W), BF16))
    out_specs.append(pl.BlockSpec((1, N_GATES, tm), lambda i: (i // tiles_per_seq, 0, i % tiles_per_seq)))
    out_shape.append(jax.ShapeDtypeStruct((nb, N_GATES, seq_len), F32))
    scratch = []
    if fourier:
        sub = tm // DFT_RADIX
        out_specs.append(pl.BlockSpec((1, DFT_RADIX, sub, F_W),
                                      lambda i: (i // tiles_per_seq, 0, i % tiles_per_seq, 0)))
        out_shape.append(jax.ShapeDtypeStruct((nb, DFT_RADIX, seq_len // DFT_RADIX, F_W), BF16))
        scratch.append(pltpu.VMEM((d // 128, tm, 128), F32))
    return pl.pallas_call(
        functools.partial(_inproj_kernel, tm=tm, fourier=fourier),
        grid=(n // tm,),
        in_specs=in_specs, out_specs=out_specs, out_shape=out_shape,
        scratch_shapes=scratch,
        compiler_params=_params(("parallel",)),
        name="inproj_latent" if fourier else "inproj_ctx",
    )(*args)


def _dft_tables(seq_len):
    sub = seq_len // DFT_RADIX
    k2 = np.arange(sub, dtype=np.int64)[None, :, None]
    n1 = np.arange(DFT_RADIX, dtype=np.int64)[:, None, None]
    n2 = np.arange(sub, dtype=np.int64)[None, None, :]
    ang = 2.0 * np.pi * ((k2 * (n1 + DFT_RADIX * n2)) % seq_len).astype(np.float64) / seq_len
    scale = 1.0 / math.sqrt(seq_len)
    return (jnp.asarray(np.cos(ang) * scale, dtype=F32).astype(BF16),
            jnp.asarray(np.sin(ang) * scale, dtype=F32).astype(BF16))


def _cadd(a, b):
    return (a[0] + b[0], a[1] + b[1])


def _csub(a, b):
    return (a[0] - b[0], a[1] - b[1])


def _cmul_w8(z, k):
    re, im = z
    r = math.sqrt(0.5)
    if k == 0:
        return z
    if k == 1:
        return ((re + im) * r, (im - re) * r)
    if k == 2:
        return (im, -re)
    return ((im - re) * r, (-re - im) * r)


def _dft8(z):
    def dft4(a, b, c, d):
        s0, s1 = _cadd(a, c), _csub(a, c)
        t0, t1 = _cadd(b, d), _csub(b, d)
        t1r = _cmul_w8(t1, 2)
        return [_cadd(s0, t0), _cadd(s1, t1r), _csub(s0, t0), _csub(s1, t1r)]
    ev = dft4(z[0], z[2], z[4], z[6])
    od = dft4(z[1], z[3], z[5], z[7])
    out = [None] * 8
    for k in range(4):
        t = _cmul_w8(od[k], k)
        out[k] = _cadd(ev[k], t)
        out[k + 4] = _csub(ev[k], t)
    return out


def _fourier_kernel(u_ref, tc_ref, ts_ref, p_ref, q_ref, ga_ref, gb_ref, *, sub, rb):
    for n1 in range(DFT_RADIX):
        un = u_ref[0, n1]
        ga_ref[n1] = _dot(tc_ref[n1], un)
        gb_ref[n1] = _dot(ts_ref[n1], un)

    def body(i, carry):
        r0 = pl.multiple_of(i * rb, rb)
        z = [(ga_ref[n1, pl.ds(r0, rb), :], -gb_ref[n1, pl.ds(r0, rb), :]) for n1 in range(DFT_RADIX)]
        x = _dft8(z)
        for k1 in range(DFT_RADIX):
            p_ref[pl.ds(k1 * sub + r0, rb), :] = x[k1][0].astype(BF16)
            q_ref[pl.ds(k1 * sub + r0, rb), :] = (-x[k1][1]).astype(BF16)
        return carry

    lax.fori_loop(0, sub // rb, body, 0)


def _fourier(u_perm, seq_len):
    nb = u_perm.shape[0]
    sub = seq_len // DFT_RADIX
    cb = 256
    tc, ts = _dft_tables(seq_len)
    return pl.pallas_call(
        functools.partial(_fourier_kernel, sub=sub, rb=32),
        grid=(nb, F_W // cb),
        in_specs=[pl.BlockSpec((1, DFT_RADIX, sub, cb), lambda b, j: (b, 0, 0, j)),
                  pl.BlockSpec((DFT_RADIX, sub, sub), lambda b, j: (0, 0, 0)),
                  pl.BlockSpec((DFT_RADIX, sub, sub), lambda b, j: (0, 0, 0))],
        out_specs=[pl.BlockSpec((seq_len, cb), lambda b, j: (b, j))] * 2,
        out_shape=[jax.ShapeDtypeStruct((nb * seq_len, F_W), BF16)] * 2,
        scratch_shapes=[pltpu.VMEM((DFT_RADIX, sub, cb), F32)] * 2,
        compiler_params=_params(("parallel", "parallel")),
        name="fourier",
    )(u_perm, tc, ts)


def _prep_kernel(xm_ref, prev_ref, next_ref, cw_ref, cb_ref, wq_ref, wkt_ref, a_ref, q_ref, kt_ref,
                 *, t, tiles_per_seq):
    i = pl.program_id(0)
    first = (i % tiles_per_seq) == 0
    last = (i % tiles_per_seq) == tiles_per_seq - 1
    x = xm_ref[...].astype(F32)
    prev_row = jnp.where(first, 0.0, prev_ref[15:16, :].astype(F32))
    next_row = jnp.where(last, 0.0, next_ref[0:1, :].astype(F32))
    rows = lax.broadcasted_iota(jnp.int32, (t, 1), 0)
    xp = jnp.where(rows == 0, prev_row, pltpu.roll(x, 1, 0))
    xn = jnp.where(rows == t - 1, next_row, pltpu.roll(x, t - 1, 0))
    cw = cw_ref[...]
    a = _silu(cw[0:1, :] * xp + cw[1:2, :] * x + cw[2:3, :] * xn + cb_ref[...])
    ab = a.astype(BF16)
    a_ref[...] = ab
    for pair in range(2):
        sl = slice(pair * 256, (pair + 1) * 256)
        q_ref[:, sl] = _dot(ab[:, sl], wq_ref[pair]).astype(BF16)
        kt = lax.dot_general(wkt_ref[pair], ab[:, sl], (((1,), (1,)), ((), ())),
                             preferred_element_type=F32)
        kt_ref[0, sl, :] = (kt * M_HEAD_DIM ** -0.5).astype(BF16)


def _pair_blockdiag(w):
    z = jnp.zeros_like(w[0])
    return jnp.stack([jnp.block([[w[0], z], [z, w[1]]]), jnp.block([[w[2], z], [z, w[3]]])])


def _prep(xm, mconv_w, mconv_b, w_q, w_k, seq_len, t):
    n = xm.shape[0]
    nb = n // seq_len
    tiles_per_seq = seq_len // t
    hb = t // 16
    nhalo = n // 16
    wq_bd = _pair_blockdiag(w_q.astype(BF16))
    wkt_bd = _pair_blockdiag(jnp.swapaxes(w_k, 1, 2).astype(BF16))
    return pl.pallas_call(
        functools.partial(_prep_kernel, t=t, tiles_per_seq=tiles_per_seq),
        grid=(n // t,),
        in_specs=[pl.BlockSpec((t, M_W), lambda i: (i, 0)),
                  pl.BlockSpec((16, M_W), lambda i: (jnp.maximum(i * hb - 1, 0), 0)),
                  pl.BlockSpec((16, M_W), lambda i: (jnp.minimum((i + 1) * hb, nhalo - 1), 0)),
                  pl.BlockSpec((3, M_W), lambda i: (0, 0)),
                  pl.BlockSpec((1, M_W), lambda i: (0, 0)),
                  pl.BlockSpec((2, 256, 256), lambda i: (0, 0, 0)),
                  pl.BlockSpec((2, 256, 256), lambda i: (0, 0, 0))],
        out_specs=[pl.BlockSpec((t, M_W), lambda i: (i, 0)),
                   pl.BlockSpec((t, M_W), lambda i: (i, 0)),
                   pl.BlockSpec((1, M_W, t), lambda i: (i // tiles_per_seq, 0, i % tiles_per_seq))],
        out_shape=[jax.ShapeDtypeStruct((n, M_W), BF16),
                   jax.ShapeDtypeStruct((n, M_W), BF16),
                   jax.ShapeDtypeStruct((nb, M_W, seq_len), BF16)],
        compiler_params=_params(("parallel",)),
        name="prep_%d" % seq_len,
    )(xm, xm, xm, mconv_w, mconv_b.reshape(1, M_W), wq_bd, wkt_bd)


def _log_sigmoid(x):
    return jnp.minimum(x, 0.0) - jnp.log1p(jnp.exp(-jnp.abs(x)))


def _tri(reverse):
    r = lax.broadcasted_iota(jnp.int32, (CHUNK, CHUNK), 0)
    c = lax.broadcasted_iota(jnp.int32, (CHUNK, CHUNK), 1)
    return (c >= r) if reverse else (c <= r)


def _cumsum_rows(gt, reverse):
    t_row = _tri(not reverse).astype(BF16)
    lf = _log_sigmoid(gt)
    l0, l1, l2 = _split3(lf)
    return lf, _dot(l0, t_row) + _dot(l1, t_row) + _dot(l2, t_row)


def _twice(x):
    return jnp.concatenate([x, x], axis=-1)


def _pair_step(q2, kts, vaugs, causal, lf_rows, brows, li_rows, c_augs, ms, with_output):
    kws, m_news, w_olds, scales, m_ts, w_inters = [], [], [], [], [], []
    for e in range(2):
        rowterm = li_rows[e] - brows[e]
        b_end = jnp.sum(lf_rows[e], axis=-1, keepdims=True)
        g = b_end + rowterm
        m_new = jnp.maximum(b_end + ms[e], jnp.max(g, axis=-1, keepdims=True))
        w_olds.append(jnp.exp(b_end + ms[e] - m_new))
        kws.append((kts[e].astype(F32) * jnp.exp(g - m_new)).astype(BF16))
        m_news.append(m_new)
        if with_output:
            bcol = jnp.sum(jnp.where(causal, lf_rows[e], 0.0), axis=-1, keepdims=True)
            dmat = jnp.where(causal, bcol + rowterm, -jnp.inf)
            inter = bcol + ms[e]
            m_t = jnp.maximum(inter, jnp.max(dmat, axis=-1, keepdims=True))
            w_inters.append(jnp.exp(inter - m_t))
            scales.append(jnp.exp(dmat - m_t))
            m_ts.append(m_t)
    zk = jnp.zeros((CHUNK, CHUNK), BF16)
    vv = jnp.concatenate(vaugs, axis=0)
    if not with_output:
        sv = _dot(jnp.concatenate([jnp.concatenate([kws[0], zk], axis=-1),
                                   jnp.concatenate([zk, kws[1]], axis=-1)], axis=0), vv)
        return [None, None], [_twice(w_olds[e]) * c_augs[e] + sv[e * CHUNK:(e + 1) * CHUNK] for e in range(2)], m_news
    zc = jnp.zeros((M_HEAD_DIM, 3 * M_HEAD_DIM), BF16)
    rhs = jnp.concatenate(
        [jnp.concatenate([kts[0], c_augs[0].astype(BF16), zc], axis=-1),
         jnp.concatenate([zc, kts[1], c_augs[1].astype(BF16)], axis=-1)], axis=0)
    qkc = _dot(q2, rhs)
    lhs = []
    for e in range(2):
        s = (qkc[:, 3 * CHUNK * e:3 * CHUNK * e + CHUNK] * scales[e]).astype(BF16)
        lhs += [jnp.concatenate([s, zk] if e == 0 else [zk, s], axis=-1),
                jnp.concatenate([kws[e], zk] if e == 0 else [zk, kws[e]], axis=-1)]
    sv = _dot(jnp.concatenate(lhs, axis=0), vv)
    hs, c_news = [], []
    for e in range(2):
        nd = _twice(w_inters[e]) * qkc[:, 3 * CHUNK * e + CHUNK:3 * CHUNK * (e + 1)] + sv[2 * e * CHUNK:(2 * e + 1) * CHUNK]
        hs.append(nd[:, :M_HEAD_DIM] / jnp.maximum(jnp.abs(nd[:, M_HEAD_DIM:]), jnp.exp(-m_ts[e])))
        c_news.append(_twice(w_olds[e]) * c_augs[e] + sv[(2 * e + 1) * CHUNK:(2 * e + 2) * CHUNK])
    return hs, c_news, m_news


def _scan_kernel(qf_ref, ktf_ref, vf_ref, gtf_ref, qb_ref, ktb_ref, vb_ref, gtb_ref,
                 ktc_ref, vc_ref, gtc_ref, hf_ref, hb_ref, c_scr, m_scr, *, ctx_chunks, nbb):
    c = pl.program_id(1)
    ones = jnp.ones((CHUNK, M_HEAD_DIM), BF16)

    def run_chunk(bb, q, kt, v, gt, d, with_output):
        reverse = d == 1
        causal = _tri(reverse)
        lf, brow = _cumsum_rows(gt, reverse)
        outs = []
        for pair in range(M_HEADS // 2):
            heads = (2 * pair, 2 * pair + 1)
            sls = [slice(hd * M_HEAD_DIM, (hd + 1) * M_HEAD_DIM) for hd in heads]
            gis = [d * 8 + hd for hd in heads]
            gfs = [d * 8 + 4 + hd for hd in heads]
            rs = [(bb * 2 + d) * M_HEADS + hd for hd in heads]
            hs, c_news, m_news = _pair_step(
                q[:, sls[0].start:sls[1].stop] if with_output else None,
                [kt[sl, :] for sl in sls], [jnp.concatenate([v[:, sl], ones], axis=-1) for sl in sls], causal,
                [lf[gf:gf + 1, :] for gf in gfs], [brow[gf:gf + 1, :] for gf in gfs],
                [gt[gi:gi + 1, :] for gi in gis], [c_scr[r] for r in rs], [m_scr[r, 0:1, :] for r in rs],
                with_output)
            for e, r in enumerate(rs):
                c_scr[r] = c_news[e]
                m_scr[r] = jnp.broadcast_to(m_news[e], (8, 128))
            outs += hs
        return outs

    @pl.when(c == 0)
    def _():
        c_scr[...] = jnp.zeros_like(c_scr)
        m_scr[...] = jnp.zeros_like(m_scr)
        for bb in range(nbb):
            for d in range(2):
                order = range(ctx_chunks) if d == 0 else range(ctx_chunks - 1, -1, -1)
                for j in order:
                    rs = slice(j * CHUNK, (j + 1) * CHUNK)
                    run_chunk(bb, None, ktc_ref[bb, :, rs], vc_ref[bb, rs, :], gtc_ref[bb, :, rs], d, False)

    for bb in range(nbb):
        hf = run_chunk(bb, qf_ref[bb], ktf_ref[bb], vf_ref[bb], gtf_ref[bb], 0, True)
        hb = run_chunk(bb, qb_ref[bb], ktb_ref[bb], vb_ref[bb], gtb_ref[bb], 1, True)
        for hd in range(M_HEADS):
            sl = slice(hd * M_HEAD_DIM, (hd + 1) * M_HEAD_DIM)
            hf_ref[bb, :, sl] = hf[hd].astype(BF16)
            hb_ref[bb, :, sl] = hb[hd].astype(BF16)


def _scan(q, kt, v, gt, kt_c, v_c, gt_c, seq_len, ctx_len, nbb):
    n = q.shape[0]
    nb = n // seq_len
    nc = seq_len // CHUNK
    q3, v3 = q.reshape(nb, seq_len, M_W), v.reshape(nb, seq_len, M_W)
    fwd = lambda b, c: (b, c, 0)
    bwd = lambda b, c: (b, nc - 1 - c, 0)
    fwd_t = lambda b, c: (b, 0, c)
    bwd_t = lambda b, c: (b, 0, nc - 1 - c)
    tokb = lambda im: pl.BlockSpec((nbb, CHUNK, M_W), im)

    def side(tok_map, tok_map_t):
        return [tokb(tok_map), pl.BlockSpec((nbb, M_W, CHUNK), tok_map_t), tokb(tok_map),
                pl.BlockSpec((nbb, N_GATES, CHUNK), tok_map_t)]

    in_specs = (side(fwd, fwd_t) + side(bwd, bwd_t)
                + [pl.BlockSpec((nbb, M_W, ctx_len), lambda b, c: (b, 0, 0)),
                   pl.BlockSpec((nbb, ctx_len, M_W), lambda b, c: (b, 0, 0)),
                   pl.BlockSpec((nbb, N_GATES, ctx_len), lambda b, c: (b, 0, 0))])
    hf, hb = pl.pallas_call(
        functools.partial(_scan_kernel, ctx_chunks=ctx_len // CHUNK, nbb=nbb),
        grid=(nb // nbb, nc),
        in_specs=in_specs,
        out_specs=[tokb(fwd), tokb(bwd)],
        out_shape=[jax.ShapeDtypeStruct((nb, seq_len, M_W), BF16)] * 2,
        scratch_shapes=[pltpu.VMEM((nbb * 2 * M_HEADS, M_HEAD_DIM, 2 * M_HEAD_DIM), F32),
                        pltpu.VMEM((nbb * 2 * M_HEADS, 8, 128), F32)],
        compiler_params=_params(("parallel", "arbitrary")),
        name="mlstm_scan",
    )(q3, kt, v3, gt, q3, kt, v3, gt, kt_c, v_c.reshape(nb, ctx_len, M_W), gt_c)
    return hf.reshape(n, M_W), hb.reshape(n, M_W)


def _mixout_kernel(x_ref, mod_ref, p_ref, q_ref, hf_ref, hb_ref, a_ref, z_ref, mnw_ref, msk_ref,
                   wc_ref, wo_ref, n2w_ref, x1_ref, h2_ref):
    mod = mod_ref[0]
    gate1, shift2, scale2 = mod[2:3, :], mod[3:4, :], mod[4:5, :]
    yf = _dot(p_ref[...], wc_ref[0]) + _dot(q_ref[...], wc_ref[1])
    h = hf_ref[...].astype(F32) + hb_ref[...].astype(F32)
    parts = []
    for hd in range(M_HEADS):
        hh = h[:, hd * M_HEAD_DIM:(hd + 1) * M_HEAD_DIM]
        mu = jnp.mean(hh, axis=-1, keepdims=True)
        dlt = hh - mu
        var = jnp.mean(dlt * dlt, axis=-1, keepdims=True)
        parts.append(dlt * lax.rsqrt(var + NORM_EPS))
    hn = jnp.concatenate(parts, axis=-1) * mnw_ref[...]
    ym = (hn + msk_ref[...] * a_ref[...].astype(F32)) * _silu(z_ref[...].astype(F32))
    y = _dot(yf.astype(BF16), wo_ref[0]) + _dot(ym.astype(BF16), wo_ref[1])
    x1 = x_ref[...] + gate1 * y
    x1_ref[...] = x1
    h2_ref[...] = (_rms(x1, n2w_ref[...]) * (1.0 + scale2) + shift2).astype(BF16)


def _channel_dft():
    k = np.arange(F_GROUP_W, dtype=np.int64)
    ang = 2.0 * np.pi * ((k[:, None] * k[None, :]) % F_GROUP_W).astype(np.float64) / F_GROUP_W
    scale = 1.0 / math.sqrt(F_GROUP_W)
    eye = np.eye(F_GROUPS)
    wc = np.stack([np.kron(eye, np.cos(ang) * scale), -np.kron(eye, np.sin(ang) * scale)])
    return jnp.asarray(wc, dtype=F32).astype(BF16)


def _mixout(x2d, mod, p, q, hf, hb, a, z, mnorm_w, m_skip, w_out, norm2_w, seq_len, tm):
    n, d = x2d.shape
    tiles_per_seq = seq_len // tm
    tok = lambda i: (i, 0)
    half = pl.BlockSpec((tm, F_W), tok)
    full = pl.BlockSpec((tm, d), tok)
    vec = lambda w: pl.BlockSpec((1, w), lambda i: (0, 0))
    return pl.pallas_call(
        _mixout_kernel,
        grid=(n // tm,),
        in_specs=[full, pl.BlockSpec((1, 6, d), lambda i: (i // tiles_per_seq, 0, 0)),
                  half, half, half, half, half, half, vec(M_W), vec(M_W),
                  pl.BlockSpec((2, F_W, F_W), lambda i: (0, 0, 0)),
                  pl.BlockSpec((2, F_W, d), lambda i: (0, 0, 0)),
                  vec(d)],
        out_specs=[full, full],
        out_shape=[jax.ShapeDtypeStruct((n, d), F32), jax.ShapeDtypeStruct((n, d), BF16)],
        compiler_params=_params(("parallel",)),
        name="mixout",
    )(x2d, mod, p, q, hf, hb, a, z, mnorm_w.reshape(1, M_W), m_skip.reshape(1, M_W),
      _channel_dft(), w_out.astype(BF16).reshape(2, F_W, d), norm2_w.reshape(1, d))


def _ffn_kernel(h_ref, hp_ref, hn_ref, x1_ref, mod_ref, wu_ref, bu_ref, cw_ref, cb_ref, wd_ref, bd_ref,
                fw_ref, o_ref, hcat_scr, ua_scr, ub_scr, act_scr, *, rows, tiles, cw):
    t = pl.program_id(1)
    nt = rows * GRID_W
    nj = D_FF // cw
    hcat_scr[0:GRID_W] = hp_ref[...]
    hcat_scr[GRID_W:GRID_W + nt] = h_ref[...]
    hcat_scr[GRID_W + nt:] = hn_ref[...]
    row8 = lax.broadcasted_iota(jnp.int32, (8, 1), 0)
    o_ref[...] = jnp.zeros_like(o_ref)

    def up(j, u_scr):
        u = _dot(hcat_scr[...], wu_ref[j]) + bu_ref[j]
        for p in range(rows + 2):
            rs = slice(p * GRID_W, (p + 1) * GRID_W)
            piece = u[rs]
            if p == 0:
                piece = jnp.where(t > 0, piece, 0.0)
            if p == rows + 1:
                piece = jnp.where(t < tiles - 1, piece, 0.0)
            prev = pltpu.roll(piece, 1, 0)
            prev = jnp.concatenate([jnp.where(row8 == 0, 0.0, prev[0:8]), prev[8:]], axis=0)
            nxt = pltpu.roll(piece, GRID_W - 1, 0)
            nxt = jnp.concatenate([nxt[:GRID_W - 8], jnp.where(row8 == 7, 0.0, nxt[GRID_W - 8:])], axis=0)
            u_scr[0, rs] = prev.astype(CONV_DTYPE)
            u_scr[1, rs] = piece.astype(CONV_DTYPE)
            u_scr[2, rs] = nxt.astype(CONV_DTYPE)

    def consume(j, u_scr):
        w = cw_ref[j]
        for r in range(rows):
            conv = cb_ref[j]
            for dr in range(3):
                rs = slice((r + dr) * GRID_W, (r + dr + 1) * GRID_W)
                for dc in range(3):
                    conv = conv + w[3 * dr + dc:3 * dr + dc + 1, :] * u_scr[dc, rs]
            act_scr[r * GRID_W:(r + 1) * GRID_W, :] = (conv[:, :cw] * _silu(conv[:, cw:])).astype(BF16)
        o_ref[...] += _dot(act_scr[...], wd_ref[j])

    up(0, ua_scr)

    def body(jj, carry):
        j = 2 * jj
        up(j + 1, ub_scr)
        consume(j, ua_scr)
        up(j + 2, ua_scr)
        consume(j + 1, ub_scr)
        return carry

    lax.fori_loop(0, nj // 2 - 1, body, 0)
    up(nj - 1, ub_scr)
    consume(nj - 2, ua_scr)
    consume(nj - 1, ub_scr)
    mod = mod_ref[0]
    y = o_ref[...] + bd_ref[...]
    o_ref[...] = _rms(x1_ref[...] + mod[5:6, :] * y, fw_ref[...])


def _pair_cast_kernel(val_ref, gate_ref, o_ref, *, cw):
    o_ref[0, :, :cw] = val_ref[...].astype(BF16)
    o_ref[0, :, cw:] = gate_ref[...].astype(BF16)


def _pair_cast(w_up, cw):
    d = w_up.shape[0]
    nj = D_FF // cw
    return pl.pallas_call(
        functools.partial(_pair_cast_kernel, cw=cw),
        grid=(nj,),
        in_specs=[pl.BlockSpec((d, cw), lambda j: (0, j)), pl.BlockSpec((d, cw), lambda j: (0, nj + j))],
        out_specs=pl.BlockSpec((1, d, 2 * cw), lambda j: (j, 0, 0)),
        out_shape=jax.ShapeDtypeStruct((nj, d, 2 * cw), BF16),
        compiler_params=_params(("parallel",)),
        name="pair_cast",
    )(w_up, w_up)


def _ffn(h2, x1, mod, w_up, b_up, fconv_w, fconv_b, w_down, b_down, final_norm_w, seq_len, rows, cw):
    n, d = x1.shape
    nb = n // seq_len
    grid_h = seq_len // GRID_W
    tiles = grid_h // rows
    nt = rows * GRID_W
    nj = D_FF // cw

    def pair(w):
        lead = w.shape[:-1]
        w2 = w.reshape(lead + (2, nj, cw))
        w2 = jnp.moveaxis(w2, -2, 0)
        return w2.reshape((nj,) + lead + (2 * cw,))

    wu = _pair_cast(w_up, cw)
    bu = pair(b_up.reshape(1, 2 * D_FF))
    cwt = pair(fconv_w.reshape(9, 2 * D_FF)).astype(CONV_DTYPE)
    cbt = pair(fconv_b.reshape(1, 2 * D_FF)).astype(CONV_DTYPE)
    wd = w_down.astype(BF16).reshape(nj, cw, d)
    nblk = n // GRID_W
    const3 = lambda b, t: (0, 0, 0)
    return pl.pallas_call(
        functools.partial(_ffn_kernel, rows=rows, tiles=tiles, cw=cw),
        grid=(nb, tiles),
        in_specs=[pl.BlockSpec((nt, d), lambda b, t: (b * tiles + t, 0)),
                  pl.BlockSpec((GRID_W, d), lambda b, t: (jnp.maximum((b * tiles + t) * rows - 1, 0), 0)),
                  pl.BlockSpec((GRID_W, d), lambda b, t: (jnp.minimum((b * tiles + t + 1) * rows, nblk - 1), 0)),
                  pl.BlockSpec((nt, d), lambda b, t: (b * tiles + t, 0)),
                  pl.BlockSpec((1, 6, d), lambda b, t: (b, 0, 0)),
                  pl.BlockSpec((nj, d, 2 * cw), const3, pipeline_mode=pl.Buffered(1)),
                  pl.BlockSpec((nj, 1, 2 * cw), const3),
                  pl.BlockSpec((nj, 9, 2 * cw), const3),
                  pl.BlockSpec((nj, 1, 2 * cw), const3),
                  pl.BlockSpec((nj, cw, d), const3, pipeline_mode=pl.Buffered(1)),
                  pl.BlockSpec((1, d), lambda b, t: (0, 0)),
                  pl.BlockSpec((1, d), lambda b, t: (0, 0))],
        out_specs=pl.BlockSpec((nt, d), lambda b, t: (b * tiles + t, 0)),
        out_shape=jax.ShapeDtypeStruct((n, d), F32),
        scratch_shapes=[pltpu.VMEM((nt + 2 * GRID_W, d), BF16),
                        pltpu.VMEM((3, nt + 2 * GRID_W, 2 * cw), CONV_DTYPE),
                        pltpu.VMEM((3, nt + 2 * GRID_W, 2 * cw), CONV_DTYPE),
                        pltpu.VMEM((nt, cw), BF16)],
        compiler_params=_params(("parallel", "parallel")),
        name="conv_ffn",
    )(h2, h2, h2, x1, mod, wu, bu, cwt, cbt, wd, b_down.reshape(1, d), final_norm_w.reshape(1, d))


def kernel(x, c, ctx, c_ctx, w_ada, b_ada, norm1_w, w_in, mconv_w, mconv_b, w_q, w_k, b_gate, mnorm_w, m_skip,
           w_out, norm2_w, w_up, b_up, fconv_w, fconv_b, w_down, b_down, final_norm_w):
    bsz, seq_len, d = x.shape
    ctx_len = ctx.shape[1]
    assert w_ada.shape[0] == 1, "single-layer kernel"
    cond = jnp.concatenate([c, c_ctx[None, :], jnp.zeros((16 - bsz - 1, d), F32)], axis=0)
    mod = _adaln(cond, w_ada[0], b_ada[0]).reshape(16, 6, d)

    x2d = x.reshape(bsz * seq_len, d)
    ctx2d = ctx.reshape(bsz * ctx_len, d)

    xm_c, v_c, gt_c = _inproj(ctx2d, mod, lambda i: bsz, norm1_w[0], w_in[0], b_gate[0],
                              ctx_len, ctx_len, False)
    _, _, kt_c = _prep(xm_c, mconv_w[0], mconv_b[0], w_q[0], w_k[0], ctx_len, ctx_len)

    tm = 512
    xm, v, z, gt, u_perm = _inproj(x2d, mod, lambda i: i // (seq_len // tm), norm1_w[0], w_in[0],
                                   b_gate[0], seq_len, tm, True)
    p, q_im = _fourier(u_perm, seq_len)
    a, q, kt = _prep(xm, mconv_w[0], mconv_b[0], w_q[0], w_k[0], seq_len, 512)
    hf, hb = _scan(q, kt, v, gt, kt_c, v_c, gt_c, seq_len, ctx_len, SCAN_SEQS)
    x1, h2 = _mixout(x2d, mod, p, q_im, hf, hb, a, z, mnorm_w[0], m_skip[0], w_out[0], norm2_w[0],
                     seq_len, tm)
    out = _ffn(h2, x1, mod, w_up[0], b_up[0], fconv_w[0], fconv_b[0], w_down[0], b_down[0],
               final_norm_w, seq_len, FFN_GRID_ROWS, FFN_CHANNELS)
    return out.reshape(bsz, seq_len, d)
```

```python
import functools
import math

import numpy as np
import jax
import jax.numpy as jnp
from jax import lax
from jax.experimental import pallas as pl
from jax.experimental.pallas import tpu as pltpu

F32 = jnp.float32
BF16 = jnp.bfloat16

D_MODEL = 1024
GRID_W = 64
F_W = 512
F_GROUPS = 4
F_GROUP_W = F_W // F_GROUPS
M_W = 512
M_HEADS = 4
M_HEAD_DIM = M_W // M_HEADS
N_GATES = 16
MLSTM_END = F_W + 2 * M_W + N_GATES
CHUNK = 128
D_FF = 5 * D_MODEL // 2
NORM_EPS = 1e-6
LOG2_E = 1.4426950408889634

DFT_RADIX = 8
VMEM_LIMIT = 56 * 1024 * 1024
INPROJ_TILE = 1024
MIXOUT_TILE = 512
PREP_TILE = 2048
SCAN_SEQS = 4
FFN_GRID_ROWS = 16
FFN_CHANNELS = 256
CONV_DTYPE = BF16


def _params(sem, vmem=VMEM_LIMIT):
    return pltpu.CompilerParams(dimension_semantics=sem, vmem_limit_bytes=vmem)


def _split3(a):
    hi = a.astype(BF16)
    r1 = a - hi.astype(F32)
    mid = r1.astype(BF16)
    lo = (r1 - mid.astype(F32)).astype(BF16)
    return hi, mid, lo


def _dot(a, b):
    return jnp.dot(a, b, preferred_element_type=F32)


def _dot_f32(a, b):
    a0, a1, a2 = _split3(a)
    b0, b1, b2 = _split3(b)
    return (_dot(a0, b0) + (_dot(a0, b1) + _dot(a1, b0))
            + (_dot(a1, b1) + _dot(a0, b2) + _dot(a2, b0)))


def _silu(x):
    return x * jax.nn.sigmoid(x)


def _rms(x, w):
    return x * lax.rsqrt(jnp.mean(x * x, axis=-1, keepdims=True) + NORM_EPS) * w


def _adaln_kernel(cond_ref, w_ref, b_ref, o_ref):
    s = _silu(cond_ref[...])
    o_ref[...] = _dot_f32(s, w_ref[...]) + b_ref[...]


def _adaln(cond, w_ada, b_ada):
    rows, d = cond.shape
    n = w_ada.shape[1]
    bn = 1536
    return pl.pallas_call(
        _adaln_kernel,
        grid=(n // bn,),
        in_specs=[pl.BlockSpec((rows, d), lambda j: (0, 0)),
                  pl.BlockSpec((d, bn), lambda j: (0, j)),
                  pl.BlockSpec((1, bn), lambda j: (0, j))],
        out_specs=pl.BlockSpec((rows, bn), lambda j: (0, j)),
        out_shape=jax.ShapeDtypeStruct((rows, n), F32),
        compiler_params=_params(("parallel",)),
        name="adaln",
    )(cond, w_ada, b_ada.reshape(1, n))


def _inproj_kernel(x_ref, mod_ref, nw_ref, wm_ref, wgt_ref, bgt_ref, *rest, tm, fourier):
    if fourier:
        wf_ref, xm_ref, v_ref, z_ref, gt_ref, u_ref, h_scr = rest
    else:
        xm_ref, v_ref, gt_ref = rest
    mod = mod_ref[0]
    shift, scale = mod[0:1, :], mod[1:2, :]
    h = _rms(x_ref[...], nw_ref[...]) * (1.0 + scale) + shift
    hb = h.astype(BF16)
    p = _dot(hb, wm_ref[...])
    xm_ref[...] = p[:, 0:M_W].astype(BF16)
    v_ref[...] = p[:, M_W:2 * M_W].astype(BF16)
    gt_ref[0] = lax.dot_general(wgt_ref[...], hb, (((1,), (1,)), ((), ())),
                                preferred_element_type=F32) + bgt_ref[...]
    if fourier:
        z_ref[...] = p[:, 2 * M_W:3 * M_W].astype(BF16)
        nlb = h.shape[1] // 128
        for lb in range(nlb):
            h_scr[lb] = h[:, lb * 128:(lb + 1) * 128]
        sub = tm // DFT_RADIX
        hs = jnp.concatenate(
            [jnp.concatenate([h_scr[lb, pl.ds(n1, sub, stride=DFT_RADIX), :] for lb in range(nlb)], axis=1)
             for n1 in range(DFT_RADIX)], axis=0)
        uf = _dot(hs.astype(BF16), wf_ref[...])
        for n1 in range(DFT_RADIX):
            u_ref[0, n1] = uf[n1 * sub:(n1 + 1) * sub, :].astype(BF16)


def _inproj(x2d, mod, mod_row, norm_w, w_in, b_gate, seq_len, tm, fourier):
    n, d = x2d.shape
    tiles_per_seq = seq_len // tm
    nb = n // seq_len
    w_in_b = w_in.astype(BF16)
    w_xm_v = w_in_b[:, F_W:F_W + 2 * M_W]
    w_gt = w_in_b[:, F_W + 2 * M_W:MLSTM_END].T
    if fourier:
        w_main = jnp.concatenate([w_xm_v, w_in_b[:, MLSTM_END:]], axis=1)
    else:
        w_main = w_xm_v
    nm = w_main.shape[1]
    tok = lambda i: (i, 0)
    const = lambda i: (0, 0)
    in_specs = [pl.BlockSpec((tm, d), tok),
                pl.BlockSpec((1, 6, d), lambda i: (mod_row(i), 0, 0)),
                pl.BlockSpec((1, d), const),
                pl.BlockSpec((d, nm), const),
                pl.BlockSpec((N_GATES, d), const),
                pl.BlockSpec((N_GATES, 1), const)]
    args = [x2d, mod, norm_w.reshape(1, d), w_main, w_gt, b_gate.reshape(N_GATES, 1)]
    tokb = pl.BlockSpec((tm, M_W), tok)
    out_specs = [tokb, tokb]
    out_shape = [jax.ShapeDtypeStruct((n, M_W), BF16)] * 2
    if fourier:
        in_specs.append(pl.BlockSpec((d, F_W), const))
        args.append(w_in_b[:, :F_W])
        out_specs.append(tokb)
        out_shape.append(jax.ShapeDtypeStruct((n, M_W), BF16))
    out_specs.append(pl.BlockSpec((1, N_GATES, tm), lambda i: (i // tiles_per_seq, 0, i % tiles_per_seq)))
    out_shape.append(jax.ShapeDtypeStruct((nb, N_GATES, seq_len), F32))
    scratch = []
    if fourier:
        sub = tm // DFT_RADIX
        out_specs.append(pl.BlockSpec((1, DFT_RADIX, sub, F_W),
                                      lambda i: (i // tiles_per_seq, 0, i % tiles_per_seq, 0)))
        out_shape.append(jax.ShapeDtypeStruct((nb, DFT_RADIX, seq_len // DFT_RADIX, F_W), BF16))
        scratch.append(pltpu.VMEM((d // 128, tm, 128), F32))
    return pl.pallas_call(
        functools.partial(_inproj_kernel, tm=tm, fourier=fourier),
        grid=(n // tm,),
        in_specs=in_specs, out_specs=out_specs, out_shape=out_shape,
        scratch_shapes=scratch,
        compiler_params=_params(("parallel",)),
        name="inproj_latent" if fourier else "inproj_ctx",
    )(*args)


def _dft_tables(seq_len):
    sub = seq_len // DFT_RADIX
    k2 = np.arange(sub, dtype=np.int64)[None, :, None]
    n1 = np.arange(DFT_RADIX, dtype=np.int64)[:, None, None]
    n2 = np.arange(sub, dtype=np.int64)[None, None, :]
    ang = 2.0 * np.pi * ((k2 * (n1 + DFT_RADIX * n2)) % seq_len).astype(np.float64) / seq_len
    scale = 1.0 / math.sqrt(seq_len)
    return (jnp.asarray(np.cos(ang) * scale, dtype=F32).astype(BF16),
            jnp.asarray(np.sin(ang) * scale, dtype=F32).astype(BF16))


def _cadd(a, b):
    return (a[0] + b[0], a[1] + b[1])


def _csub(a, b):
    return (a[0] - b[0], a[1] - b[1])


def _cmul_w8(z, k):
    re, im = z
    r = math.sqrt(0.5)
    if k == 0:
        return z
    if k == 1:
        return ((re + im) * r, (im - re) * r)
    if k == 2:
        return (im, -re)
    return ((im - re) * r, (-re - im) * r)


def _dft8(z):
    def dft4(a, b, c, d):
        s0, s1 = _cadd(a, c), _csub(a, c)
        t0, t1 = _cadd(b, d), _csub(b, d)
        t1r = _cmul_w8(t1, 2)
        return [_cadd(s0, t0), _cadd(s1, t1r), _csub(s0, t0), _csub(s1, t1r)]
    ev = dft4(z[0], z[2], z[4], z[6])
    od = dft4(z[1], z[3], z[5], z[7])
    out = [None] * 8
    for k in range(4):
        t = _cmul_w8(od[k], k)
        out[k] = _cadd(ev[k], t)
        out[k + 4] = _csub(ev[k], t)
    return out


def _fourier_kernel(u_ref, tc_ref, ts_ref, p_ref, q_ref, ga_ref, gb_ref, *, sub, rb):
    for n1 in range(DFT_RADIX):
        un = u_ref[0, n1]
        ga_ref[n1] = _dot(tc_ref[n1], un)
        gb_ref[n1] = _dot(ts_ref[n1], un)

    def body(i, carry):
        r0 = pl.multiple_of(i * rb, rb)
        z = [(ga_ref[n1, pl.ds(r0, rb), :], -gb_ref[n1, pl.ds(r0, rb), :]) for n1 in range(DFT_RADIX)]
        x = _dft8(z)
        for k1 in range(DFT_RADIX):
            p_ref[pl.ds(k1 * sub + r0, rb), :] = x[k1][0].astype(BF16)
            q_ref[pl.ds(k1 * sub + r0, rb), :] = (-x[k1][1]).astype(BF16)
        return carry

    lax.fori_loop(0, sub // rb, body, 0)


def _fourier(u_perm, seq_len):
    nb = u_perm.shape[0]
    sub = seq_len // DFT_RADIX
    cb = 256
    tc, ts = _dft_tables(seq_len)
    return pl.pallas_call(
        functools.partial(_fourier_kernel, sub=sub, rb=32),
        grid=(nb, F_W // cb),
        in_specs=[pl.BlockSpec((1, DFT_RADIX, sub, cb), lambda b, j: (b, 0, 0, j)),
                  pl.BlockSpec((DFT_RADIX, sub, sub), lambda b, j: (0, 0, 0)),
                  pl.BlockSpec((DFT_RADIX, sub, sub), lambda b, j: (0, 0, 0))],
        out_specs=[pl.BlockSpec((seq_len, cb), lambda b, j: (b, j))] * 2,
        out_shape=[jax.ShapeDtypeStruct((nb * seq_len, F_W), BF16)] * 2,
        scratch_shapes=[pltpu.VMEM((DFT_RADIX, sub, cb), F32)] * 2,
        compiler_params=_params(("parallel", "parallel")),
        name="fourier",
    )(u_perm, tc, ts)


def _prep_kernel(xm_ref, prev_ref, next_ref, cw_ref, cb_ref, wq_ref, wkt_ref, a_ref, q_ref, kt_ref,
                 *, t, tiles_per_seq):
    i = pl.program_id(0)
    first = (i % tiles_per_seq) == 0
    last = (i % tiles_per_seq) == tiles_per_seq - 1
    x = xm_ref[...].astype(F32)
    prev_row = jnp.where(first, 0.0, prev_ref[15:16, :].astype(F32))
    next_row = jnp.where(last, 0.0, next_ref[0:1, :].astype(F32))
    rows = lax.broadcasted_iota(jnp.int32, (t, 1), 0)
    xp = jnp.where(rows == 0, prev_row, pltpu.roll(x, 1, 0))
    xn = jnp.where(rows == t - 1, next_row, pltpu.roll(x, t - 1, 0))
    cw = cw_ref[...]
    a = _silu(cw[0:1, :] * xp + cw[1:2, :] * x + cw[2:3, :] * xn + cb_ref[...])
    ab = a.astype(BF16)
    a_ref[...] = ab
    for pair in range(2):
        sl = slice(pair * 256, (pair + 1) * 256)
        q_ref[:, sl] = _dot(ab[:, sl], wq_ref[pair]).astype(BF16)
        kt = lax.dot_general(wkt_ref[pair], ab[:, sl], (((1,), (1,)), ((), ())),
                             preferred_element_type=F32)
        kt_ref[0, sl, :] = (kt * M_HEAD_DIM ** -0.5).astype(BF16)


def _pair_blockdiag(w):
    z = jnp.zeros_like(w[0])
    return jnp.stack([jnp.block([[w[0], z], [z, w[1]]]), jnp.block([[w[2], z], [z, w[3]]])])


def _prep(xm, mconv_w, mconv_b, w_q, w_k, seq_len, t):
    n = xm.shape[0]
    nb = n // seq_len
    tiles_per_seq = seq_len // t
    hb = t // 16
    nhalo = n // 16
    wq_bd = _pair_blockdiag(w_q.astype(BF16))
    wkt_bd = _pair_blockdiag(jnp.swapaxes(w_k, 1, 2).astype(BF16))
    return pl.pallas_call(
        functools.partial(_prep_kernel, t=t, tiles_per_seq=tiles_per_seq),
        grid=(n // t,),
        in_specs=[pl.BlockSpec((t, M_W), lambda i: (i, 0)),
                  pl.BlockSpec((16, M_W), lambda i: (jnp.maximum(i * hb - 1, 0), 0)),
                  pl.BlockSpec((16, M_W), lambda i: (jnp.minimum((i + 1) * hb, nhalo - 1), 0)),
                  pl.BlockSpec((3, M_W), lambda i: (0, 0)),
                  pl.BlockSpec((1, M_W), lambda i: (0, 0)),
                  pl.BlockSpec((2, 256, 256), lambda i: (0, 0, 0)),
                  pl.BlockSpec((2, 256, 256), lambda i: (0, 0, 0))],
        out_specs=[pl.BlockSpec((t, M_W), lambda i: (i, 0)),
                   pl.BlockSpec((t, M_W), lambda i: (i, 0)),
                   pl.BlockSpec((1, M_W, t), lambda i: (i // tiles_per_seq, 0, i % tiles_per_seq))],
        out_shape=[jax.ShapeDtypeStruct((n, M_W), BF16),
                   jax.ShapeDtypeStruct((n, M_W), BF16),
                   jax.ShapeDtypeStruct((nb, M_W, seq_len), BF16)],
        compiler_params=_params(("parallel",)),
        name="prep_%d" % seq_len,
    )(xm, xm, xm, mconv_w, mconv_b.reshape(1, M_W), wq_bd, wkt_bd)


def _log_sigmoid(x):
    return jnp.minimum(x, 0.0) - jnp.log1p(jnp.exp(-jnp.abs(x)))


def _tri(reverse):
    r = lax.broadcasted_iota(jnp.int32, (CHUNK, CHUNK), 0)
    c = lax.broadcasted_iota(jnp.int32, (CHUNK, CHUNK), 1)
    return (c >= r) if reverse else (c <= r)


def _cumsum_rows(gt, reverse):
    t_row = _tri(not reverse).astype(BF16)
    lf = _log_sigmoid(gt) * LOG2_E
    l0, l1, l2 = _split3(lf)
    return lf, _dot(l0, t_row) + _dot(l1, t_row) + _dot(l2, t_row)


def _twice(x):
    return jnp.concatenate([x, x], axis=-1)


def _pair_step(q2, kts, vaugs, causal, lf_rows, brows, li_rows, c_augs, ms, with_output):
    kws, m_news, w_olds, scales, m_ts, w_inters = [], [], [], [], [], []
    for e in range(2):
        rowterm = li_rows[e] - brows[e]
        b_end = jnp.sum(lf_rows[e], axis=-1, keepdims=True)
        g = b_end + rowterm
        m_new = jnp.maximum(b_end + ms[e], jnp.max(g, axis=-1, keepdims=True))
        w_olds.append(jnp.exp2(b_end + ms[e] - m_new))
        kws.append((kts[e].astype(F32) * jnp.exp2(g - m_new)).astype(BF16))
        m_news.append(m_new)
        if with_output:
            bcol = jnp.sum(jnp.where(causal, lf_rows[e], 0.0), axis=-1, keepdims=True)
            dmat = jnp.where(causal, bcol + rowterm, -jnp.inf)
            inter = bcol + ms[e]
            m_t = jnp.maximum(inter, jnp.max(dmat, axis=-1, keepdims=True))
            w_inters.append(jnp.exp2(inter - m_t))
            scales.append(jnp.exp2(dmat - m_t))
            m_ts.append(m_t)
    zk = jnp.zeros((CHUNK, CHUNK), BF16)
    vv = jnp.concatenate(vaugs, axis=0)
    if not with_output:
        sv = _dot(jnp.concatenate([jnp.concatenate([kws[0], zk], axis=-1),
                                   jnp.concatenate([zk, kws[1]], axis=-1)], axis=0), vv)
        return [None, None], [_twice(w_olds[e]) * c_augs[e] + sv[e * CHUNK:(e + 1) * CHUNK] for e in range(2)], m_news
    zc = jnp.zeros((M_HEAD_DIM, 3 * M_HEAD_DIM), BF16)
    rhs = jnp.concatenate(
        [jnp.concatenate([kts[0], c_augs[0].astype(BF16), zc], axis=-1),
         jnp.concatenate([zc, kts[1], c_augs[1].astype(BF16)], axis=-1)], axis=0)
    qkc = _dot(q2, rhs)
    lhs = []
    for e in range(2):
        s = (qkc[:, 3 * CHUNK * e:3 * CHUNK * e + CHUNK] * scales[e]).astype(BF16)
        lhs += [jnp.concatenate([s, zk] if e == 0 else [zk, s], axis=-1),
                jnp.concatenate([kws[e], zk] if e == 0 else [zk, kws[e]], axis=-1)]
    sv = _dot(jnp.concatenate(lhs, axis=0), vv)
    hs, c_news = [], []
    for e in range(2):
        nd = _twice(w_inters[e]) * qkc[:, 3 * CHUNK * e + CHUNK:3 * CHUNK * (e + 1)] + sv[2 * e * CHUNK:(2 * e + 1) * CHUNK]
        hs.append(nd[:, :M_HEAD_DIM] / jnp.maximum(jnp.abs(nd[:, M_HEAD_DIM:]), jnp.exp2(-m_ts[e])))
        c_news.append(_twice(w_olds[e]) * c_augs[e] + sv[(2 * e + 1) * CHUNK:(2 * e + 2) * CHUNK])
    return hs, c_news, m_news


def _scan_kernel(qf_ref, ktf_ref, vf_ref, gtf_ref, qb_ref, ktb_ref, vb_ref, gtb_ref,
                 ktc_ref, vc_ref, gtc_ref, hf_ref, hb_ref, c_scr, m_scr, *, ctx_chunks, nbb):
    c = pl.program_id(1)
    ones = jnp.ones((CHUNK, M_HEAD_DIM), BF16)

    def run_chunk(bb, q, kt, v, gt, d, with_output):
        reverse = d == 1
        causal = _tri(reverse)
        lf, brow = _cumsum_rows(gt, reverse)
        outs = []
        for pair in range(M_HEADS // 2):
            heads = (2 * pair, 2 * pair + 1)
            sls = [slice(hd * M_HEAD_DIM, (hd + 1) * M_HEAD_DIM) for hd in heads]
            gis = [d * 8 + hd for hd in heads]
            gfs = [d * 8 + 4 + hd for hd in heads]
            rs = [(bb * 2 + d) * M_HEADS + hd for hd in heads]
            hs, c_news, m_news = _pair_step(
                q[:, sls[0].start:sls[1].stop] if with_output else None,
                [kt[sl, :] for sl in sls], [jnp.concatenate([v[:, sl], ones], axis=-1) for sl in sls], causal,
                [lf[gf:gf + 1, :] for gf in gfs], [brow[gf:gf + 1, :] for gf in gfs],
                [gt[gi:gi + 1, :] * LOG2_E for gi in gis], [c_scr[r] for r in rs], [m_scr[r, 0:1, :] for r in rs],
                with_output)
            for e, r in enumerate(rs):
                c_scr[r] = c_news[e]
                m_scr[r] = jnp.broadcast_to(m_news[e], (8, 128))
            outs += hs
        return outs

    @pl.when(c == 0)
    def _():
        c_scr[...] = jnp.zeros_like(c_scr)
        m_scr[...] = jnp.zeros_like(m_scr)
        for bb in range(nbb):
            for d in range(2):
                order = range(ctx_chunks) if d == 0 else range(ctx_chunks - 1, -1, -1)
                for j in order:
                    rs = slice(j * CHUNK, (j + 1) * CHUNK)
                    run_chunk(bb, None, ktc_ref[bb, :, rs], vc_ref[bb, rs, :], gtc_ref[bb, :, rs], d, False)

    for bb in range(nbb):
        hf = run_chunk(bb, qf_ref[bb], ktf_ref[bb], vf_ref[bb], gtf_ref[bb], 0, True)
        hb = run_chunk(bb, qb_ref[bb], ktb_ref[bb], vb_ref[bb], gtb_ref[bb], 1, True)
        for hd in range(M_HEADS):
            sl = slice(hd * M_HEAD_DIM, (hd + 1) * M_HEAD_DIM)
            hf_ref[bb, :, sl] = hf[hd].astype(BF16)
            hb_ref[bb, :, sl] = hb[hd].astype(BF16)


def _scan(q, kt, v, gt, kt_c, v_c, gt_c, seq_len, ctx_len, nbb):
    n = q.shape[0]
    nb = n // seq_len
    nc = seq_len // CHUNK
    q3, v3 = q.reshape(nb, seq_len, M_W), v.reshape(nb, seq_len, M_W)
    fwd = lambda b, c: (b, c, 0)
    bwd = lambda b, c: (b, nc - 1 - c, 0)
    fwd_t = lambda b, c: (b, 0, c)
    bwd_t = lambda b, c: (b, 0, nc - 1 - c)
    tokb = lambda im: pl.BlockSpec((nbb, CHUNK, M_W), im)

    def side(tok_map, tok_map_t):
        return [tokb(tok_map), pl.BlockSpec((nbb, M_W, CHUNK), tok_map_t), tokb(tok_map),
                pl.BlockSpec((nbb, N_GATES, CHUNK), tok_map_t)]

    in_specs = (side(fwd, fwd_t) + side(bwd, bwd_t)
                + [pl.BlockSpec((nbb, M_W, ctx_len), lambda b, c: (b, 0, 0)),
                   pl.BlockSpec((nbb, ctx_len, M_W), lambda b, c: (b, 0, 0)),
                   pl.BlockSpec((nbb, N_GATES, ctx_len), lambda b, c: (b, 0, 0))])
    hf, hb = pl.pallas_call(
        functools.partial(_scan_kernel, ctx_chunks=ctx_len // CHUNK, nbb=nbb),
        grid=(nb // nbb, nc),
        in_specs=in_specs,
        out_specs=[tokb(fwd), tokb(bwd)],
        out_shape=[jax.ShapeDtypeStruct((nb, seq_len, M_W), BF16)] * 2,
        scratch_shapes=[pltpu.VMEM((nbb * 2 * M_HEADS, M_HEAD_DIM, 2 * M_HEAD_DIM), F32),
                        pltpu.VMEM((nbb * 2 * M_HEADS, 8, 128), F32)],
        compiler_params=_params(("parallel", "arbitrary")),
        name="mlstm_scan",
    )(q3, kt, v3, gt, q3, kt, v3, gt, kt_c, v_c.reshape(nb, ctx_len, M_W), gt_c)
    return hf.reshape(n, M_W), hb.reshape(n, M_W)


def _mixout_kernel(x_ref, mod_ref, p_ref, q_ref, hf_ref, hb_ref, a_ref, z_ref, mnw_ref, msk_ref,
                   wc_ref, wo_ref, n2w_ref, x1_ref, h2_ref):
    mod = mod_ref[0]
    gate1, shift2, scale2 = mod[2:3, :], mod[3:4, :], mod[4:5, :]
    yf = _dot(p_ref[...], wc_ref[0]) + _dot(q_ref[...], wc_ref[1])
    h = hf_ref[...].astype(F32) + hb_ref[...].astype(F32)
    parts = []
    for hd in range(M_HEADS):
        hh = h[:, hd * M_HEAD_DIM:(hd + 1) * M_HEAD_DIM]
        mu = jnp.mean(hh, axis=-1, keepdims=True)
        dlt = hh - mu
        var = jnp.mean(dlt * dlt, axis=-1, keepdims=True)
        parts.append(dlt * lax.rsqrt(var + NORM_EPS))
    hn = jnp.concatenate(parts, axis=-1) * mnw_ref[...]
    ym = (hn + msk_ref[...] * a_ref[...].astype(F32)) * _silu(z_ref[...].astype(F32))
    y = _dot(yf.astype(BF16), wo_ref[0]) + _dot(ym.astype(BF16), wo_ref[1])
    x1 = x_ref[...] + gate1 * y
    x1_ref[...] = x1
    h2_ref[...] = (_rms(x1, n2w_ref[...]) * (1.0 + scale2) + shift2).astype(BF16)


def _channel_dft():
    k = np.arange(F_GROUP_W, dtype=np.int64)
    ang = 2.0 * np.pi * ((k[:, None] * k[None, :]) % F_GROUP_W).astype(np.float64) / F_GROUP_W
    scale = 1.0 / math.sqrt(F_GROUP_W)
    eye = np.eye(F_GROUPS)
    wc = np.stack([np.kron(eye, np.cos(ang) * scale), -np.kron(eye, np.sin(ang) * scale)])
    return jnp.asarray(wc, dtype=F32).astype(BF16)


def _mixout(x2d, mod, p, q, hf, hb, a, z, mnorm_w, m_skip, w_out, norm2_w, seq_len, tm):
    n, d = x2d.shape
    tiles_per_seq = seq_len // tm
    tok = lambda i: (i, 0)
    half = pl.BlockSpec((tm, F_W), tok)
    full = pl.BlockSpec((tm, d), tok)
    vec = lambda w: pl.BlockSpec((1, w), lambda i: (0, 0))
    return pl.pallas_call(
        _mixout_kernel,
        grid=(n // tm,),
        in_specs=[full, pl.BlockSpec((1, 6, d), lambda i: (i // tiles_per_seq, 0, 0)),
                  half, half, half, half, half, half, vec(M_W), vec(M_W),
                  pl.BlockSpec((2, F_W, F_W), lambda i: (0, 0, 0)),
                  pl.BlockSpec((2, F_W, d), lambda i: (0, 0, 0)),
                  vec(d)],
        out_specs=[full, full],
        out_shape=[jax.ShapeDtypeStruct((n, d), F32), jax.ShapeDtypeStruct((n, d), BF16)],
        compiler_params=_params(("parallel",)),
        name="mixout",
    )(x2d, mod, p, q, hf, hb, a, z, mnorm_w.reshape(1, M_W), m_skip.reshape(1, M_W),
      _channel_dft(), w_out.astype(BF16).reshape(2, F_W, d), norm2_w.reshape(1, d))


def _ffn_kernel(h_ref, hp_ref, hn_ref, x1_ref, mod_ref, wu_ref, bu_ref, cw_ref, cb_ref, wd_ref, bd_ref,
                fw_ref, o_ref, hcat_scr, ua_scr, ub_scr, act_scr, *, rows, tiles, cw):
    t = pl.program_id(1)
    nt = rows * GRID_W
    nj = D_FF // cw
    hcat_scr[0:GRID_W] = hp_ref[...]
    hcat_scr[GRID_W:GRID_W + nt] = h_ref[...]
    hcat_scr[GRID_W + nt:] = hn_ref[...]
    row8 = lax.broadcasted_iota(jnp.int32, (8, 1), 0)
    o_ref[...] = jnp.zeros_like(o_ref)

    def up(j, u_scr):
        u = _dot(hcat_scr[...], wu_ref[j]) + bu_ref[j]
        for p in range(rows + 2):
            rs = slice(p * GRID_W, (p + 1) * GRID_W)
            piece = u[rs]
            if p == 0:
                piece = jnp.where(t > 0, piece, 0.0)
            if p == rows + 1:
                piece = jnp.where(t < tiles - 1, piece, 0.0)
            prev = pltpu.roll(piece, 1, 0)
            prev = jnp.concatenate([jnp.where(row8 == 0, 0.0, prev[0:8]), prev[8:]], axis=0)
            nxt = pltpu.roll(piece, GRID_W - 1, 0)
            nxt = jnp.concatenate([nxt[:GRID_W - 8], jnp.where(row8 == 7, 0.0, nxt[GRID_W - 8:])], axis=0)
            u_scr[0, rs] = prev.astype(CONV_DTYPE)
            u_scr[1, rs] = piece.astype(CONV_DTYPE)
            u_scr[2, rs] = nxt.astype(CONV_DTYPE)

    def consume(j, u_scr):
        w = cw_ref[j]
        for r in range(rows):
            conv = cb_ref[j]
            for dr in range(3):
                rs = slice((r + dr) * GRID_W, (r + dr + 1) * GRID_W)
                for dc in range(3):
                    conv = conv + w[3 * dr + dc:3 * dr + dc + 1, :] * u_scr[dc, rs]
            act_scr[r * GRID_W:(r + 1) * GRID_W, :] = (conv[:, :cw] * _silu(conv[:, cw:])).astype(BF16)
        o_ref[...] += _dot(act_scr[...], wd_ref[j])

    up(0, ua_scr)

    def body(jj, carry):
        j = 2 * jj
        up(j + 1, ub_scr)
        consume(j, ua_scr)
        up(j + 2, ua_scr)
        consume(j + 1, ub_scr)
        return carry

    lax.fori_loop(0, nj // 2 - 1, body, 0)
    up(nj - 1, ub_scr)
    consume(nj - 2, ua_scr)
    consume(nj - 1, ub_scr)
    y = o_ref[...] + bd_ref[...]
    o_ref[...] = _rms(x1_ref[...] + mod_ref[0][5:6, :] * y, fw_ref[...])


def _pair_cast_kernel(val_ref, gate_ref, o_ref, *, cw):
    o_ref[0, :, :cw] = val_ref[...].astype(BF16)
    o_ref[0, :, cw:] = gate_ref[...].astype(BF16)


def _pair_cast(w_up, cw):
    d = w_up.shape[0]
    nj = D_FF // cw
    return pl.pallas_call(
        functools.partial(_pair_cast_kernel, cw=cw),
        grid=(nj,),
        in_specs=[pl.BlockSpec((d, cw), lambda j: (0, j)), pl.BlockSpec((d, cw), lambda j: (0, nj + j))],
        out_specs=pl.BlockSpec((1, d, 2 * cw), lambda j: (j, 0, 0)),
        out_shape=jax.ShapeDtypeStruct((nj, d, 2 * cw), BF16),
        compiler_params=_params(("parallel",)),
        name="pair_cast",
    )(w_up, w_up)


def _ffn(h2, x1, mod, w_up, b_up, fconv_w, fconv_b, w_down, b_down, final_norm_w, seq_len, rows, cw):
    n, d = x1.shape
    nb = n // seq_len
    grid_h = seq_len // GRID_W
    tiles = grid_h // rows
    nt = rows * GRID_W
    nj = D_FF // cw

    def pair(w):
        lead = w.shape[:-1]
        w2 = w.reshape(lead + (2, nj, cw))
        w2 = jnp.moveaxis(w2, -2, 0)
        return w2.reshape((nj,) + lead + (2 * cw,))

    wu = _pair_cast(w_up, cw)
    bu = pair(b_up.reshape(1, 2 * D_FF))
    cwt = pair(fconv_w.reshape(9, 2 * D_FF)).astype(CONV_DTYPE)
    cbt = pair(fconv_b.reshape(1, 2 * D_FF)).astype(CONV_DTYPE)
    wd = w_down.astype(BF16).reshape(nj, cw, d)
    nblk = n // GRID_W
    const3 = lambda b, t: (0, 0, 0)
    return pl.pallas_call(
        functools.partial(_ffn_kernel, rows=rows, tiles=tiles, cw=cw),
        grid=(nb, tiles),
        in_specs=[pl.BlockSpec((nt, d), lambda b, t: (b * tiles + t, 0)),
                  pl.BlockSpec((GRID_W, d), lambda b, t: (jnp.maximum((b * tiles + t) * rows - 1, 0), 0)),
                  pl.BlockSpec((GRID_W, d), lambda b, t: (jnp.minimum((b * tiles + t + 1) * rows, nblk - 1), 0)),
                  pl.BlockSpec((nt, d), lambda b, t: (b * tiles + t, 0)),
                  pl.BlockSpec((1, 6, d), lambda b, t: (b, 0, 0)),
                  pl.BlockSpec((nj, d, 2 * cw), const3, pipeline_mode=pl.Buffered(1)),
                  pl.BlockSpec((nj, 1, 2 * cw), const3),
                  pl.BlockSpec((nj, 9, 2 * cw), const3),
                  pl.BlockSpec((nj, 1, 2 * cw), const3),
                  pl.BlockSpec((nj, cw, d), const3, pipeline_mode=pl.Buffered(1)),
                  pl.BlockSpec((1, d), lambda b, t: (0, 0)),
                  pl.BlockSpec((1, d), lambda b, t: (0, 0))],
        out_specs=pl.BlockSpec((nt, d), lambda b, t: (b * tiles + t, 0)),
        out_shape=jax.ShapeDtypeStruct((n, d), F32),
        scratch_shapes=[pltpu.VMEM((nt + 2 * GRID_W, d), BF16),
                        pltpu.VMEM((3, nt + 2 * GRID_W, 2 * cw), CONV_DTYPE),
                        pltpu.VMEM((3, nt + 2 * GRID_W, 2 * cw), CONV_DTYPE),
                        pltpu.VMEM((nt, cw), BF16)],
        compiler_params=_params(("parallel", "parallel")),
        name="conv_ffn",
    )(h2, h2, h2, x1, mod, wu, bu, cwt, cbt, wd, b_down.reshape(1, d), final_norm_w.reshape(1, d))


def kernel(x, c, ctx, c_ctx, w_ada, b_ada, norm1_w, w_in, mconv_w, mconv_b, w_q, w_k, b_gate, mnorm_w, m_skip,
           w_out, norm2_w, w_up, b_up, fconv_w, fconv_b, w_down, b_down, final_norm_w):
    bsz, seq_len, d = x.shape
    ctx_len = ctx.shape[1]
    assert w_ada.shape[0] == 1, "single-layer kernel"
    cond = jnp.concatenate([c, c_ctx[None, :], jnp.zeros((16 - bsz - 1, d), F32)], axis=0)
    mod = _adaln(cond, w_ada[0], b_ada[0]).reshape(16, 6, d)

    x2d = x.reshape(bsz * seq_len, d)
    ctx2d = ctx.reshape(bsz * ctx_len, d)

    xm_c, v_c, gt_c = _inproj(ctx2d, mod, lambda i: bsz, norm1_w[0], w_in[0], b_gate[0],
                              ctx_len, ctx_len, False)
    _, _, kt_c = _prep(xm_c, mconv_w[0], mconv_b[0], w_q[0], w_k[0], ctx_len, ctx_len)

    xm, v, z, gt, u_perm = _inproj(x2d, mod, lambda i: i // (seq_len // INPROJ_TILE), norm1_w[0], w_in[0],
                                   b_gate[0], seq_len, INPROJ_TILE, True)
    p, q_im = _fourier(u_perm, seq_len)
    a, q, kt = _prep(xm, mconv_w[0], mconv_b[0], w_q[0], w_k[0], seq_len, PREP_TILE)
    hf, hb = _scan(q, kt, v, gt, kt_c, v_c, gt_c, seq_len, ctx_len, SCAN_SEQS)
    x1, h2 = _mixout(x2d, mod, p, q_im, hf, hb, a, z, mnorm_w[0], m_skip[0], w_out[0], norm2_w[0],
                     seq_len, MIXOUT_TILE)
    out = _ffn(h2, x1, mod, w_up[0], b_up[0], fconv_w[0], fconv_b[0], w_down[0], b_down[0],
               final_norm_w, seq_len, FFN_GRID_ROWS, FFN_CHANNELS)
    return out.reshape(bsz, seq_len, d)
```

```python
import functools
import math

import numpy as np
import jax
import jax.numpy as jnp
from jax import lax
from jax.experimental import pallas as pl
from jax.experimental.pallas import tpu as pltpu

F32 = jnp.float32
BF16 = jnp.bfloat16

D_MODEL = 1024
GRID_W = 64
F_W = 512
F_GROUPS = 4
F_GROUP_W = F_W // F_GROUPS
M_W = 512
M_HEADS = 4
M_HEAD_DIM = M_W // M_HEADS
N_GATES = 16
MLSTM_END = F_W + 2 * M_W + N_GATES
CHUNK = 128
D_FF = 5 * D_MODEL // 2
NORM_EPS = 1e-6
LOG2_E = 1.4426950408889634

DFT_RADIX = 8
VMEM_LIMIT = 56 * 1024 * 1024
INPROJ_TILE = 1024
MIXOUT_TILE = 1024
PREP_TILE = 2048
SCAN_SEQS = 4
FFN_GRID_ROWS = 16
FFN_CHANNELS = 256
CONV_DTYPE = BF16


def _params(sem, vmem=VMEM_LIMIT):
    return pltpu.CompilerParams(dimension_semantics=sem, vmem_limit_bytes=vmem)


def _split3(a):
    hi = a.astype(BF16)
    r1 = a - hi.astype(F32)
    mid = r1.astype(BF16)
    lo = (r1 - mid.astype(F32)).astype(BF16)
    return hi, mid, lo


def _dot(a, b):
    return jnp.dot(a, b, preferred_element_type=F32)


def _dot_f32(a, b):
    a0, a1, a2 = _split3(a)
    b0, b1, b2 = _split3(b)
    return (_dot(a0, b0) + (_dot(a0, b1) + _dot(a1, b0))
            + (_dot(a1, b1) + _dot(a0, b2) + _dot(a2, b0)))


def _silu(x):
    return x * jax.nn.sigmoid(x)


def _rms(x, w):
    return x * lax.rsqrt(jnp.mean(x * x, axis=-1, keepdims=True) + NORM_EPS) * w


def _adaln_kernel(cond_ref, w_ref, b_ref, o_ref):
    s = _silu(cond_ref[...])
    o_ref[...] = _dot_f32(s, w_ref[...]) + b_ref[...]


def _adaln(cond, w_ada, b_ada):
    rows, d = cond.shape
    n = w_ada.shape[1]
    bn = 1536
    return pl.pallas_call(
        _adaln_kernel,
        grid=(n // bn,),
        in_specs=[pl.BlockSpec((rows, d), lambda j: (0, 0)),
                  pl.BlockSpec((d, bn), lambda j: (0, j)),
                  pl.BlockSpec((1, bn), lambda j: (0, j))],
        out_specs=pl.BlockSpec((rows, bn), lambda j: (0, j)),
        out_shape=jax.ShapeDtypeStruct((rows, n), F32),
        compiler_params=_params(("parallel",)),
        name="adaln",
    )(cond, w_ada, b_ada.reshape(1, n))


def _inproj_kernel(x_ref, mod_ref, nw_ref, wm_ref, wgt_ref, bgt_ref, *rest, tm, fourier):
    if fourier:
        wf_ref, xm_ref, v_ref, z_ref, gt_ref, u_ref, h_scr = rest
    else:
        xm_ref, v_ref, gt_ref = rest
    mod = mod_ref[0]
    shift, scale = mod[0:1, :], mod[1:2, :]
    h = _rms(x_ref[...], nw_ref[...]) * (1.0 + scale) + shift
    hb = h.astype(BF16)
    p = _dot(hb, wm_ref[...])
    xm_ref[...] = p[:, 0:M_W].astype(BF16)
    v_ref[...] = p[:, M_W:2 * M_W].astype(BF16)
    gt_ref[0] = lax.dot_general(wgt_ref[...], hb, (((1,), (1,)), ((), ())),
                                preferred_element_type=F32) + bgt_ref[...]
    if fourier:
        z_ref[...] = p[:, 2 * M_W:3 * M_W].astype(BF16)
        nlb = h.shape[1] // 128
        for lb in range(nlb):
            h_scr[lb] = h[:, lb * 128:(lb + 1) * 128]
        sub = tm // DFT_RADIX
        hs = jnp.concatenate(
            [jnp.concatenate([h_scr[lb, pl.ds(n1, sub, stride=DFT_RADIX), :] for lb in range(nlb)], axis=1)
             for n1 in range(DFT_RADIX)], axis=0)
        uf = _dot(hs.astype(BF16), wf_ref[...])
        for n1 in range(DFT_RADIX):
            u_ref[0, n1] = uf[n1 * sub:(n1 + 1) * sub, :].astype(BF16)


def _inproj(x2d, mod, mod_row, norm_w, w_in, b_gate, seq_len, tm, fourier):
    n, d = x2d.shape
    tiles_per_seq = seq_len // tm
    nb = n // seq_len
    w_in_b = w_in.astype(BF16)
    w_xm_v = w_in_b[:, F_W:F_W + 2 * M_W]
    w_gt = w_in_b[:, F_W + 2 * M_W:MLSTM_END].T
    if fourier:
        w_main = jnp.concatenate([w_xm_v, w_in_b[:, MLSTM_END:]], axis=1)
    else:
        w_main = w_xm_v
    nm = w_main.shape[1]
    tok = lambda i: (i, 0)
    const = lambda i: (0, 0)
    in_specs = [pl.BlockSpec((tm, d), tok),
                pl.BlockSpec((1, 6, d), lambda i: (mod_row(i), 0, 0)),
                pl.BlockSpec((1, d), const),
                pl.BlockSpec((d, nm), const),
                pl.BlockSpec((N_GATES, d), const),
                pl.BlockSpec((N_GATES, 1), const)]
    args = [x2d, mod, norm_w.reshape(1, d), w_main, w_gt, b_gate.reshape(N_GATES, 1)]
    tokb = pl.BlockSpec((tm, M_W), tok)
    out_specs = [tokb, tokb]
    out_shape = [jax.ShapeDtypeStruct((n, M_W), BF16)] * 2
    if fourier:
        in_specs.append(pl.BlockSpec((d, F_W), const))
        args.append(w_in_b[:, :F_W])
        out_specs.append(tokb)
        out_shape.append(jax.ShapeDtypeStruct((n, M_W), BF16))
    out_specs.append(pl.BlockSpec((1, N_GATES, tm), lambda i: (i // tiles_per_seq, 0, i % tiles_per_seq)))
    out_shape.append(jax.ShapeDtypeStruct((nb, N_GATES, seq_len), F32))
    scratch = []
    if fourier:
        sub = tm // DFT_RADIX
        out_specs.append(pl.BlockSpec((1, DFT_RADIX, sub, F_W),
                                      lambda i: (i // tiles_per_seq, 0, i % tiles_per_seq, 0)))
        out_shape.append(jax.ShapeDtypeStruct((nb, DFT_RADIX, seq_len // DFT_RADIX, F_W), BF16))
        scratch.append(pltpu.VMEM((d // 128, tm, 128), F32))
    return pl.pallas_call(
        functools.partial(_inproj_kernel, tm=tm, fourier=fourier),
        grid=(n // tm,),
        in_specs=in_specs, out_specs=out_specs, out_shape=out_shape,
        scratch_shapes=scratch,
        compiler_params=_params(("parallel",)),
        name="inproj_latent" if fourier else "inproj_ctx",
    )(*args)


def _dft_tables(seq_len):
    sub = seq_len // DFT_RADIX
    k2 = np.arange(sub, dtype=np.int64)[None, :, None]
    n1 = np.arange(DFT_RADIX, dtype=np.int64)[:, None, None]
    n2 = np.arange(sub, dtype=np.int64)[None, None, :]
    ang = 2.0 * np.pi * ((k2 * (n1 + DFT_RADIX * n2)) % seq_len).astype(np.float64) / seq_len
    scale = 1.0 / math.sqrt(seq_len)
    return (jnp.asarray(np.cos(ang) * scale, dtype=F32).astype(BF16),
            jnp.asarray(np.sin(ang) * scale, dtype=F32).astype(BF16))


def _cadd(a, b):
    return (a[0] + b[0], a[1] + b[1])


def _csub(a, b):
    return (a[0] - b[0], a[1] - b[1])


def _cmul_w8(z, k):
    re, im = z
    r = math.sqrt(0.5)
    if k == 0:
        return z
    if k == 1:
        return ((re + im) * r, (im - re) * r)
    if k == 2:
        return (im, -re)
    return ((im - re) * r, (-re - im) * r)


def _dft8(z):
    def dft4(a, b, c, d):
        s0, s1 = _cadd(a, c), _csub(a, c)
        t0, t1 = _cadd(b, d), _csub(b, d)
        t1r = _cmul_w8(t1, 2)
        return [_cadd(s0, t0), _cadd(s1, t1r), _csub(s0, t0), _csub(s1, t1r)]
    ev = dft4(z[0], z[2], z[4], z[6])
    od = dft4(z[1], z[3], z[5], z[7])
    out = [None] * 8
    for k in range(4):
        t = _cmul_w8(od[k], k)
        out[k] = _cadd(ev[k], t)
        out[k + 4] = _csub(ev[k], t)
    return out


def _fourier_kernel(u_ref, tcs_ref, p_ref, q_ref, ga_ref, gb_ref, *, sub, rb):
    for n1 in range(DFT_RADIX):
        g = _dot(tcs_ref[n1], u_ref[0, n1])
        ga_ref[n1] = g[:sub]
        gb_ref[n1] = g[sub:]

    def body(i, carry):
        r0 = pl.multiple_of(i * rb, rb)
        z = [(ga_ref[n1, pl.ds(r0, rb), :], -gb_ref[n1, pl.ds(r0, rb), :]) for n1 in range(DFT_RADIX)]
        x = _dft8(z)
        for k1 in range(DFT_RADIX):
            p_ref[pl.ds(k1 * sub + r0, rb), :] = x[k1][0].astype(BF16)
            q_ref[pl.ds(k1 * sub + r0, rb), :] = (-x[k1][1]).astype(BF16)
        return carry

    lax.fori_loop(0, sub // rb, body, 0)


def _fourier(u_perm, seq_len):
    nb = u_perm.shape[0]
    sub = seq_len // DFT_RADIX
    cb = 256
    tcs = jnp.concatenate(_dft_tables(seq_len), axis=1)
    return pl.pallas_call(
        functools.partial(_fourier_kernel, sub=sub, rb=32),
        grid=(nb, F_W // cb),
        in_specs=[pl.BlockSpec((1, DFT_RADIX, sub, cb), lambda b, j: (b, 0, 0, j)),
                  pl.BlockSpec((DFT_RADIX, 2 * sub, sub), lambda b, j: (0, 0, 0))],
        out_specs=[pl.BlockSpec((seq_len, cb), lambda b, j: (b, j))] * 2,
        out_shape=[jax.ShapeDtypeStruct((nb * seq_len, F_W), BF16)] * 2,
        scratch_shapes=[pltpu.VMEM((DFT_RADIX, sub, cb), F32)] * 2,
        compiler_params=_params(("parallel", "parallel")),
        name="fourier",
    )(u_perm, tcs)


def _prep_kernel(xm_ref, prev_ref, next_ref, cw_ref, cb_ref, wq_ref, wkt_ref, a_ref, q_ref, kt_ref,
                 *, t, tiles_per_seq):
    i = pl.program_id(0)
    first = (i % tiles_per_seq) == 0
    last = (i % tiles_per_seq) == tiles_per_seq - 1
    x = xm_ref[...].astype(F32)
    prev_row = jnp.where(first, 0.0, prev_ref[15:16, :].astype(F32))
    next_row = jnp.where(last, 0.0, next_ref[0:1, :].astype(F32))
    rows = lax.broadcasted_iota(jnp.int32, (t, 1), 0)
    xp = jnp.where(rows == 0, prev_row, pltpu.roll(x, 1, 0))
    xn = jnp.where(rows == t - 1, next_row, pltpu.roll(x, t - 1, 0))
    cw = cw_ref[...]
    a = _silu(cw[0:1, :] * xp + cw[1:2, :] * x + cw[2:3, :] * xn + cb_ref[...])
    ab = a.astype(BF16)
    a_ref[...] = ab
    for pair in range(2):
        sl = slice(pair * 256, (pair + 1) * 256)
        q_ref[:, sl] = _dot(ab[:, sl], wq_ref[pair]).astype(BF16)
        kt = lax.dot_general(wkt_ref[pair], ab[:, sl], (((1,), (1,)), ((), ())),
                             preferred_element_type=F32)
        kt_ref[0, sl, :] = (kt * M_HEAD_DIM ** -0.5).astype(BF16)


def _pair_blockdiag(w):
    z = jnp.zeros_like(w[0])
    return jnp.stack([jnp.block([[w[0], z], [z, w[1]]]), jnp.block([[w[2], z], [z, w[3]]])])


def _prep(xm, mconv_w, mconv_b, w_q, w_k, seq_len, t):
    n = xm.shape[0]
    nb = n // seq_len
    tiles_per_seq = seq_len // t
    hb = t // 16
    nhalo = n // 16
    wq_bd = _pair_blockdiag(w_q.astype(BF16))
    wkt_bd = _pair_blockdiag(jnp.swapaxes(w_k, 1, 2).astype(BF16))
    return pl.pallas_call(
        functools.partial(_prep_kernel, t=t, tiles_per_seq=tiles_per_seq),
        grid=(n // t,),
        in_specs=[pl.BlockSpec((t, M_W), lambda i: (i, 0)),
                  pl.BlockSpec((16, M_W), lambda i: (jnp.maximum(i * hb - 1, 0), 0)),
                  pl.BlockSpec((16, M_W), lambda i: (jnp.minimum((i + 1) * hb, nhalo - 1), 0)),
                  pl.BlockSpec((3, M_W), lambda i: (0, 0)),
                  pl.BlockSpec((1, M_W), lambda i: (0, 0)),
                  pl.BlockSpec((2, 256, 256), lambda i: (0, 0, 0)),
                  pl.BlockSpec((2, 256, 256), lambda i: (0, 0, 0))],
        out_specs=[pl.BlockSpec((t, M_W), lambda i: (i, 0)),
                   pl.BlockSpec((t, M_W), lambda i: (i, 0)),
                   pl.BlockSpec((1, M_W, t), lambda i: (i // tiles_per_seq, 0, i % tiles_per_seq))],
        out_shape=[jax.ShapeDtypeStruct((n, M_W), BF16),
                   jax.ShapeDtypeStruct((n, M_W), BF16),
                   jax.ShapeDtypeStruct((nb, M_W, seq_len), BF16)],
        compiler_params=_params(("parallel",)),
        name="prep_%d" % seq_len,
    )(xm, xm, xm, mconv_w, mconv_b.reshape(1, M_W), wq_bd, wkt_bd)


def _log_sigmoid(x):
    return jnp.minimum(x, 0.0) - jnp.log1p(jnp.exp(-jnp.abs(x)))


def _tri(reverse):
    r = lax.broadcasted_iota(jnp.int32, (CHUNK, CHUNK), 0)
    c = lax.broadcasted_iota(jnp.int32, (CHUNK, CHUNK), 1)
    return (c >= r) if reverse else (c <= r)


def _cumsum_rows(gt, reverse):
    t_row = _tri(not reverse).astype(BF16)
    lf = _log_sigmoid(gt) * LOG2_E
    l0, l1, l2 = _split3(lf)
    return lf, _dot(l0, t_row) + _dot(l1, t_row) + _dot(l2, t_row)


def _twice(x):
    return jnp.concatenate([x, x], axis=-1)


def _pair_step(q2, kts, vaugs, causal, lf_rows, brows, li_rows, c_augs, ms, with_output):
    kws, m_news, w_olds, scales, m_ts, w_inters = [], [], [], [], [], []
    for e in range(2):
        rowterm = li_rows[e] - brows[e]
        b_end = jnp.sum(lf_rows[e], axis=-1, keepdims=True)
        g = b_end + rowterm
        m_new = jnp.maximum(b_end + ms[e], jnp.max(g, axis=-1, keepdims=True))
        w_olds.append(jnp.exp2(b_end + ms[e] - m_new))
        kws.append((kts[e].astype(F32) * jnp.exp2(g - m_new)).astype(BF16))
        m_news.append(m_new)
        if with_output:
            bcol = jnp.sum(jnp.where(causal, lf_rows[e], 0.0), axis=-1, keepdims=True)
            dmat = jnp.where(causal, bcol + rowterm, -jnp.inf)
            inter = bcol + ms[e]
            m_t = jnp.maximum(inter, jnp.max(dmat, axis=-1, keepdims=True))
            w_inters.append(jnp.exp2(inter - m_t))
            scales.append(jnp.exp2(dmat - m_t))
            m_ts.append(m_t)
    zk = jnp.zeros((CHUNK, CHUNK), BF16)
    vv = jnp.concatenate(vaugs, axis=0)
    if not with_output:
        sv = _dot(jnp.concatenate([jnp.concatenate([kws[0], zk], axis=-1),
                                   jnp.concatenate([zk, kws[1]], axis=-1)], axis=0), vv)
        return [None, None], [_twice(w_olds[e]) * c_augs[e] + sv[e * CHUNK:(e + 1) * CHUNK] for e in range(2)], m_news
    zc = jnp.zeros((M_HEAD_DIM, 3 * M_HEAD_DIM), BF16)
    rhs = jnp.concatenate(
        [jnp.concatenate([kts[0], c_augs[0].astype(BF16), zc], axis=-1),
         jnp.concatenate([zc, kts[1], c_augs[1].astype(BF16)], axis=-1)], axis=0)
    qkc = _dot(q2, rhs)
    lhs = []
    for e in range(2):
        s = (qkc[:, 3 * CHUNK * e:3 * CHUNK * e + CHUNK] * scales[e]).astype(BF16)
        lhs += [jnp.concatenate([s, zk] if e == 0 else [zk, s], axis=-1),
                jnp.concatenate([kws[e], zk] if e == 0 else [zk, kws[e]], axis=-1)]
    sv = _dot(jnp.concatenate(lhs, axis=0), vv)
    hs, c_news = [], []
    for e in range(2):
        nd = _twice(w_inters[e]) * qkc[:, 3 * CHUNK * e + CHUNK:3 * CHUNK * (e + 1)] + sv[2 * e * CHUNK:(2 * e + 1) * CHUNK]
        hs.append(nd[:, :M_HEAD_DIM] / jnp.maximum(jnp.abs(nd[:, M_HEAD_DIM:]), jnp.exp2(-m_ts[e])))
        c_news.append(_twice(w_olds[e]) * c_augs[e] + sv[(2 * e + 1) * CHUNK:(2 * e + 2) * CHUNK])
    return hs, c_news, m_news


def _scan_kernel(qf_ref, ktf_ref, vf_ref, gtf_ref, qb_ref, ktb_ref, vb_ref, gtb_ref,
                 ktc_ref, vc_ref, gtc_ref, hf_ref, hb_ref, c_scr, m_scr, *, ctx_chunks, nbb):
    c = pl.program_id(1)
    ones = jnp.ones((CHUNK, M_HEAD_DIM), BF16)

    def run_chunk(bb, q, kt, v, gt, d, with_output):
        reverse = d == 1
        causal = _tri(reverse)
        lf, brow = _cumsum_rows(gt, reverse)
        outs = []
        for pair in range(M_HEADS // 2):
            heads = (2 * pair, 2 * pair + 1)
            sls = [slice(hd * M_HEAD_DIM, (hd + 1) * M_HEAD_DIM) for hd in heads]
            gis = [d * 8 + hd for hd in heads]
            gfs = [d * 8 + 4 + hd for hd in heads]
            rs = [(bb * 2 + d) * M_HEADS + hd for hd in heads]
            hs, c_news, m_news = _pair_step(
                q[:, sls[0].start:sls[1].stop] if with_output else None,
                [kt[sl, :] for sl in sls], [jnp.concatenate([v[:, sl], ones], axis=-1) for sl in sls], causal,
                [lf[gf:gf + 1, :] for gf in gfs], [brow[gf:gf + 1, :] for gf in gfs],
                [gt[gi:gi + 1, :] * LOG2_E for gi in gis], [c_scr[r] for r in rs], [m_scr[r, 0:1, :] for r in rs],
                with_output)
            for e, r in enumerate(rs):
                c_scr[r] = c_news[e]
                m_scr[r] = jnp.broadcast_to(m_news[e], (8, 128))
            outs += hs
        return outs

    @pl.when(c == 0)
    def _():
        c_scr[...] = jnp.zeros_like(c_scr)
        m_scr[...] = jnp.zeros_like(m_scr)
        for bb in range(nbb):
            for d in range(2):
                order = range(ctx_chunks) if d == 0 else range(ctx_chunks - 1, -1, -1)
                for j in order:
                    rs = slice(j * CHUNK, (j + 1) * CHUNK)
                    run_chunk(bb, None, ktc_ref[bb, :, rs], vc_ref[bb, rs, :], gtc_ref[bb, :, rs], d, False)

    for bb in range(nbb):
        hf = run_chunk(bb, qf_ref[bb], ktf_ref[bb], vf_ref[bb], gtf_ref[bb], 0, True)
        hb = run_chunk(bb, qb_ref[bb], ktb_ref[bb], vb_ref[bb], gtb_ref[bb], 1, True)
        for hd in range(M_HEADS):
            sl = slice(hd * M_HEAD_DIM, (hd + 1) * M_HEAD_DIM)
            hf_ref[bb, :, sl] = hf[hd].astype(BF16)
            hb_ref[bb, :, sl] = hb[hd].astype(BF16)


def _scan(q, kt, v, gt, kt_c, v_c, gt_c, seq_len, ctx_len, nbb):
    n = q.shape[0]
    nb = n // seq_len
    nc = seq_len // CHUNK
    q3, v3 = q.reshape(nb, seq_len, M_W), v.reshape(nb, seq_len, M_W)
    fwd = lambda b, c: (b, c, 0)
    bwd = lambda b, c: (b, nc - 1 - c, 0)
    fwd_t = lambda b, c: (b, 0, c)
    bwd_t = lambda b, c: (b, 0, nc - 1 - c)
    tokb = lambda im: pl.BlockSpec((nbb, CHUNK, M_W), im)

    def side(tok_map, tok_map_t):
        return [tokb(tok_map), pl.BlockSpec((nbb, M_W, CHUNK), tok_map_t), tokb(tok_map),
                pl.BlockSpec((nbb, N_GATES, CHUNK), tok_map_t)]

    in_specs = (side(fwd, fwd_t) + side(bwd, bwd_t)
                + [pl.BlockSpec((nbb, M_W, ctx_len), lambda b, c: (b, 0, 0)),
                   pl.BlockSpec((nbb, ctx_len, M_W), lambda b, c: (b, 0, 0)),
                   pl.BlockSpec((nbb, N_GATES, ctx_len), lambda b, c: (b, 0, 0))])
    hf, hb = pl.pallas_call(
        functools.partial(_scan_kernel, ctx_chunks=ctx_len // CHUNK, nbb=nbb),
        grid=(nb // nbb, nc),
        in_specs=in_specs,
        out_specs=[tokb(fwd), tokb(bwd)],
        out_shape=[jax.ShapeDtypeStruct((nb, seq_len, M_W), BF16)] * 2,
        scratch_shapes=[pltpu.VMEM((nbb * 2 * M_HEADS, M_HEAD_DIM, 2 * M_HEAD_DIM), F32),
                        pltpu.VMEM((nbb * 2 * M_HEADS, 8, 128), F32)],
        compiler_params=_params(("parallel", "arbitrary")),
        name="mlstm_scan",
    )(q3, kt, v3, gt, q3, kt, v3, gt, kt_c, v_c.reshape(nb, ctx_len, M_W), gt_c)
    return hf.reshape(n, M_W), hb.reshape(n, M_W)


def _mixout_kernel(x_ref, mod_ref, p_ref, q_ref, hf_ref, hb_ref, a_ref, z_ref, mnw_ref, msk_ref,
                   wc_ref, wo_ref, n2w_ref, x1_ref, h2_ref):
    mod = mod_ref[0]
    gate1, shift2, scale2 = mod[2:3, :], mod[3:4, :], mod[4:5, :]
    yf = _dot(p_ref[...], wc_ref[0]) + _dot(q_ref[...], wc_ref[1])
    h = hf_ref[...].astype(F32) + hb_ref[...].astype(F32)
    parts = []
    for hd in range(M_HEADS):
        hh = h[:, hd * M_HEAD_DIM:(hd + 1) * M_HEAD_DIM]
        mu = jnp.mean(hh, axis=-1, keepdims=True)
        dlt = hh - mu
        var = jnp.mean(dlt * dlt, axis=-1, keepdims=True)
        parts.append(dlt * lax.rsqrt(var + NORM_EPS))
    hn = jnp.concatenate(parts, axis=-1) * mnw_ref[...]
    ym = (hn + msk_ref[...] * a_ref[...].astype(F32)) * _silu(z_ref[...].astype(F32))
    y = _dot(yf.astype(BF16), wo_ref[0]) + _dot(ym.astype(BF16), wo_ref[1])
    x1 = x_ref[...] + gate1 * y
    x1_ref[...] = x1
    h2_ref[...] = (_rms(x1, n2w_ref[...]) * (1.0 + scale2) + shift2).astype(BF16)


def _channel_dft():
    k = np.arange(F_GROUP_W, dtype=np.int64)
    ang = 2.0 * np.pi * ((k[:, None] * k[None, :]) % F_GROUP_W).astype(np.float64) / F_GROUP_W
    scale = 1.0 / math.sqrt(F_GROUP_W)
    eye = np.eye(F_GROUPS)
    wc = np.stack([np.kron(eye, np.cos(ang) * scale), -np.kron(eye, np.sin(ang) * scale)])
    return jnp.asarray(wc, dtype=F32).astype(BF16)


def _mixout(x2d, mod, p, q, hf, hb, a, z, mnorm_w, m_skip, w_out, norm2_w, seq_len, tm):
    n, d = x2d.shape
    tiles_per_seq = seq_len // tm
    tok = lambda i: (i, 0)
    half = pl.BlockSpec((tm, F_W), tok)
    full = pl.BlockSpec((tm, d), tok)
    vec = lambda w: pl.BlockSpec((1, w), lambda i: (0, 0))
    return pl.pallas_call(
        _mixout_kernel,
        grid=(n // tm,),
        in_specs=[full, pl.BlockSpec((1, 6, d), lambda i: (i // tiles_per_seq, 0, 0)),
                  half, half, half, half, half, half, vec(M_W), vec(M_W),
                  pl.BlockSpec((2, F_W, F_W), lambda i: (0, 0, 0)),
                  pl.BlockSpec((2, F_W, d), lambda i: (0, 0, 0)),
                  vec(d)],
        out_specs=[full, full],
        out_shape=[jax.ShapeDtypeStruct((n, d), F32), jax.ShapeDtypeStruct((n, d), BF16)],
        compiler_params=_params(("parallel",)),
        name="mixout",
    )(x2d, mod, p, q, hf, hb, a, z, mnorm_w.reshape(1, M_W), m_skip.reshape(1, M_W),
      _channel_dft(), w_out.astype(BF16).reshape(2, F_W, d), norm2_w.reshape(1, d))


def _ffn_kernel(h_ref, hp_ref, hn_ref, x1_ref, mod_ref, wu_ref, bu_ref, cw_ref, cb_ref, wd_ref, bd_ref,
                fw_ref, o_ref, hcat_scr, ua_scr, ub_scr, act_scr, *, rows, tiles, cw):
    t = pl.program_id(1)
    nt = rows * GRID_W
    nj = D_FF // cw
    hcat_scr[0:GRID_W] = hp_ref[...]
    hcat_scr[GRID_W:GRID_W + nt] = h_ref[...]
    hcat_scr[GRID_W + nt:] = hn_ref[...]
    row8 = lax.broadcasted_iota(jnp.int32, (8, 1), 0)
    o_ref[...] = jnp.zeros_like(o_ref)

    def up(j, u_scr):
        u = _dot(hcat_scr[...], wu_ref[j]) + bu_ref[j]
        for p in range(rows + 2):
            rs = slice(p * GRID_W, (p + 1) * GRID_W)
            piece = u[rs]
            if p == 0:
                piece = jnp.where(t > 0, piece, 0.0)
            if p == rows + 1:
                piece = jnp.where(t < tiles - 1, piece, 0.0)
            prev = pltpu.roll(piece, 1, 0)
            prev = jnp.concatenate([jnp.where(row8 == 0, 0.0, prev[0:8]), prev[8:]], axis=0)
            nxt = pltpu.roll(piece, GRID_W - 1, 0)
            nxt = jnp.concatenate([nxt[:GRID_W - 8], jnp.where(row8 == 7, 0.0, nxt[GRID_W - 8:])], axis=0)
            u_scr[0, rs] = prev.astype(CONV_DTYPE)
            u_scr[1, rs] = piece.astype(CONV_DTYPE)
            u_scr[2, rs] = nxt.astype(CONV_DTYPE)

    def consume(j, u_scr):
        w = cw_ref[j]
        for r in range(rows):
            conv = cb_ref[j]
            for dr in range(3):
                rs = slice((r + dr) * GRID_W, (r + dr + 1) * GRID_W)
                for dc in range(3):
                    conv = conv + w[3 * dr + dc:3 * dr + dc + 1, :] * u_scr[dc, rs]
            act_scr[r * GRID_W:(r + 1) * GRID_W, :] = (conv[:, :cw] * _silu(conv[:, cw:])).astype(BF16)
        o_ref[...] += _dot(act_scr[...], wd_ref[j])

    up(0, ua_scr)

    def body(jj, carry):
        j = 2 * jj
        up(j + 1, ub_scr)
        consume(j, ua_scr)
        up(j + 2, ua_scr)
        consume(j + 1, ub_scr)
        return carry

    for jj in range(nj // 2 - 1):
        body(jj, 0)
    up(nj - 1, ub_scr)
    consume(nj - 2, ua_scr)
    consume(nj - 1, ub_scr)
    y = o_ref[...] + bd_ref[...]
    o_ref[...] = _rms(x1_ref[...] + mod_ref[0][5:6, :] * y, fw_ref[...])


def _pair_cast_kernel(val_ref, gate_ref, o_ref, *, cw):
    o_ref[0, :, :cw] = val_ref[...].astype(BF16)
    o_ref[0, :, cw:] = gate_ref[...].astype(BF16)


def _pair_cast(w_up, cw):
    d = w_up.shape[0]
    nj = D_FF // cw
    return pl.pallas_call(
        functools.partial(_pair_cast_kernel, cw=cw),
        grid=(nj,),
        in_specs=[pl.BlockSpec((d, cw), lambda j: (0, j)), pl.BlockSpec((d, cw), lambda j: (0, nj + j))],
        out_specs=pl.BlockSpec((1, d, 2 * cw), lambda j: (j, 0, 0)),
        out_shape=jax.ShapeDtypeStruct((nj, d, 2 * cw), BF16),
        compiler_params=_params(("parallel",)),
        name="pair_cast",
    )(w_up, w_up)


def _ffn(h2, x1, mod, w_up, b_up, fconv_w, fconv_b, w_down, b_down, final_norm_w, seq_len, rows, cw):
    n, d = x1.shape
    nb = n // seq_len
    grid_h = seq_len // GRID_W
    tiles = grid_h // rows
    nt = rows * GRID_W
    nj = D_FF // cw

    def pair(w):
        lead = w.shape[:-1]
        w2 = w.reshape(lead + (2, nj, cw))
        w2 = jnp.moveaxis(w2, -2, 0)
        return w2.reshape((nj,) + lead + (2 * cw,))

    wu = _pair_cast(w_up, cw)
    bu = pair(b_up.reshape(1, 2 * D_FF))
    cwt = pair(fconv_w.reshape(9, 2 * D_FF)).astype(CONV_DTYPE)
    cbt = pair(fconv_b.reshape(1, 2 * D_FF)).astype(CONV_DTYPE)
    wd = w_down.astype(BF16).reshape(nj, cw, d)
    nblk = n // GRID_W
    const3 = lambda b, t: (0, 0, 0)
    return pl.pallas_call(
        functools.partial(_ffn_kernel, rows=rows, tiles=tiles, cw=cw),
        grid=(nb, tiles),
        in_specs=[pl.BlockSpec((nt, d), lambda b, t: (b * tiles + t, 0)),
                  pl.BlockSpec((GRID_W, d), lambda b, t: (jnp.maximum((b * tiles + t) * rows - 1, 0), 0)),
                  pl.BlockSpec((GRID_W, d), lambda b, t: (jnp.minimum((b * tiles + t + 1) * rows, nblk - 1), 0)),
                  pl.BlockSpec((nt, d), lambda b, t: (b * tiles + t, 0)),
                  pl.BlockSpec((1, 6, d), lambda b, t: (b, 0, 0)),
                  pl.BlockSpec((nj, d, 2 * cw), const3, pipeline_mode=pl.Buffered(1)),
                  pl.BlockSpec((nj, 1, 2 * cw), const3),
                  pl.BlockSpec((nj, 9, 2 * cw), const3),
                  pl.BlockSpec((nj, 1, 2 * cw), const3),
                  pl.BlockSpec((nj, cw, d), const3, pipeline_mode=pl.Buffered(1)),
                  pl.BlockSpec((1, d), lambda b, t: (0, 0)),
                  pl.BlockSpec((1, d), lambda b, t: (0, 0))],
        out_specs=pl.BlockSpec((nt, d), lambda b, t: (b * tiles + t, 0)),
        out_shape=jax.ShapeDtypeStruct((n, d), F32),
        scratch_shapes=[pltpu.VMEM((nt + 2 * GRID_W, d), BF16),
                        pltpu.VMEM((3, nt + 2 * GRID_W, 2 * cw), CONV_DTYPE),
                        pltpu.VMEM((3, nt + 2 * GRID_W, 2 * cw), CONV_DTYPE),
                        pltpu.VMEM((nt, cw), BF16)],
        compiler_params=_params(("parallel", "parallel")),
        name="conv_ffn",
    )(h2, h2, h2, x1, mod, wu, bu, cwt, cbt, wd, b_down.reshape(1, d), final_norm_w.reshape(1, d))


def kernel(x, c, ctx, c_ctx, w_ada, b_ada, norm1_w, w_in, mconv_w, mconv_b, w_q, w_k, b_gate, mnorm_w, m_skip,
           w_out, norm2_w, w_up, b_up, fconv_w, fconv_b, w_down, b_down, final_norm_w):
    bsz, seq_len, d = x.shape
    ctx_len = ctx.shape[1]
    assert w_ada.shape[0] == 1, "single-layer kernel"
    cond = jnp.concatenate([c, c_ctx[None, :], jnp.zeros((16 - bsz - 1, d), F32)], axis=0)
    mod = _adaln(cond, w_ada[0], b_ada[0]).reshape(16, 6, d)

    x2d = x.reshape(bsz * seq_len, d)
    ctx2d = ctx.reshape(bsz * ctx_len, d)

    xm_c, v_c, gt_c = _inproj(ctx2d, mod, lambda i: bsz, norm1_w[0], w_in[0], b_gate[0],
                              ctx_len, ctx_len, False)
    _, _, kt_c = _prep(xm_c, mconv_w[0], mconv_b[0], w_q[0], w_k[0], ctx_len, ctx_len)

    xm, v, z, gt, u_perm = _inproj(x2d, mod, lambda i: i // (seq_len // INPROJ_TILE), norm1_w[0], w_in[0],
                                   b_gate[0], seq_len, INPROJ_TILE, True)
    p, q_im = _fourier(u_perm, seq_len)
    a, q, kt = _prep(xm, mconv_w[0], mconv_b[0], w_q[0], w_k[0], seq_len, PREP_TILE)
    hf, hb = _scan(q, kt, v, gt, kt_c, v_c, gt_c, seq_len, ctx_len, SCAN_SEQS)
    x1, h2 = _mixout(x2d, mod, p, q_im, hf, hb, a, z, mnorm_w[0], m_skip[0], w_out[0], norm2_w[0],
                     seq_len, MIXOUT_TILE)
    out = _ffn(h2, x1, mod, w_up[0], b_up[0], fconv_w[0], fconv_b[0], w_down[0], b_down[0],
               final_norm_w, seq_len, FFN_GRID_ROWS, FFN_CHANNELS)
    return out.reshape(bsz, seq_len, d)
```

```python
import functools
import math

import numpy as np
import jax
import jax.numpy as jnp
from jax import lax
from jax.experimental import pallas as pl
from jax.experimental.pallas import tpu as pltpu

F32 = jnp.float32
BF16 = jnp.bfloat16

D_MODEL = 1024
GRID_W = 64
F_W = 512
F_GROUPS = 4
F_GROUP_W = F_W // F_GROUPS
M_W = 512
M_HEADS = 4
M_HEAD_DIM = M_W // M_HEADS
N_GATES = 16
MLSTM_END = F_W + 2 * M_W + N_GATES
CHUNK = 128
D_FF = 5 * D_MODEL // 2
NORM_EPS = 1e-6
LOG2_E = 1.4426950408889634

DFT_RADIX = 8
VMEM_LIMIT = 56 * 1024 * 1024
INPROJ_TILE = 1024
MIXOUT_TILE = 1024
PREP_TILE = 2048
SCAN_SEQS = 8
FFN_GRID_ROWS = 16
FFN_CHANNELS = 256
CONV_DTYPE = BF16


def _params(sem, vmem=VMEM_LIMIT):
    return pltpu.CompilerParams(dimension_semantics=sem, vmem_limit_bytes=vmem)


def _split3(a):
    hi = a.astype(BF16)
    r1 = a - hi.astype(F32)
    mid = r1.astype(BF16)
    lo = (r1 - mid.astype(F32)).astype(BF16)
    return hi, mid, lo


def _dot(a, b):
    return jnp.dot(a, b, preferred_element_type=F32)


def _dot_f32(a, b):
    a0, a1, a2 = _split3(a)
    b0, b1, b2 = _split3(b)
    return (_dot(a0, b0) + (_dot(a0, b1) + _dot(a1, b0))
            + (_dot(a1, b1) + _dot(a0, b2) + _dot(a2, b0)))


def _silu(x):
    return x * jax.nn.sigmoid(x)


def _rms(x, w):
    return x * lax.rsqrt(jnp.mean(x * x, axis=-1, keepdims=True) + NORM_EPS) * w


def _adaln_kernel(cond_ref, w_ref, b_ref, o_ref):
    s = _silu(cond_ref[...])
    o_ref[...] = _dot_f32(s, w_ref[...]) + b_ref[...]


def _adaln(cond, w_ada, b_ada):
    rows, d = cond.shape
    n = w_ada.shape[1]
    bn = 1536
    return pl.pallas_call(
        _adaln_kernel,
        grid=(n // bn,),
        in_specs=[pl.BlockSpec((rows, d), lambda j: (0, 0)),
                  pl.BlockSpec((d, bn), lambda j: (0, j)),
                  pl.BlockSpec((1, bn), lambda j: (0, j))],
        out_specs=pl.BlockSpec((rows, bn), lambda j: (0, j)),
        out_shape=jax.ShapeDtypeStruct((rows, n), F32),
        compiler_params=_params(("parallel",)),
        name="adaln",
    )(cond, w_ada, b_ada.reshape(1, n))


def _inproj_kernel(x_ref, mod_ref, nw_ref, wm_ref, wgt_ref, bgt_ref, *rest, tm, fourier):
    if fourier:
        wf_ref, xm_ref, v_ref, z_ref, gt_ref, u_ref, h_scr = rest
    else:
        xm_ref, v_ref, gt_ref = rest
    mod = mod_ref[0]
    shift, scale = mod[0:1, :], mod[1:2, :]
    h = _rms(x_ref[...], nw_ref[...]) * (1.0 + scale) + shift
    hb = h.astype(BF16)
    p = _dot(hb, wm_ref[...])
    xm_ref[...] = p[:, 0:M_W].astype(BF16)
    v_ref[...] = p[:, M_W:2 * M_W].astype(BF16)
    gt_ref[0] = lax.dot_general(wgt_ref[...], hb, (((1,), (1,)), ((), ())),
                                preferred_element_type=F32) + bgt_ref[...]
    if fourier:
        z_ref[...] = p[:, 2 * M_W:3 * M_W].astype(BF16)
        nlb = h.shape[1] // 128
        for lb in range(nlb):
            h_scr[lb] = h[:, lb * 128:(lb + 1) * 128]
        sub = tm // DFT_RADIX
        hs = jnp.concatenate(
            [jnp.concatenate([h_scr[lb, pl.ds(n1, sub, stride=DFT_RADIX), :] for lb in range(nlb)], axis=1)
             for n1 in range(DFT_RADIX)], axis=0)
        uf = _dot(hs.astype(BF16), wf_ref[...])
        for n1 in range(DFT_RADIX):
            u_ref[0, n1] = uf[n1 * sub:(n1 + 1) * sub, :].astype(BF16)


def _inproj(x2d, mod, mod_row, norm_w, w_in, b_gate, seq_len, tm, fourier):
    n, d = x2d.shape
    tiles_per_seq = seq_len // tm
    nb = n // seq_len
    w_in_b = w_in.astype(BF16)
    w_xm_v = w_in_b[:, F_W:F_W + 2 * M_W]
    w_gt = w_in_b[:, F_W + 2 * M_W:MLSTM_END].T
    if fourier:
        w_main = jnp.concatenate([w_xm_v, w_in_b[:, MLSTM_END:]], axis=1)
    else:
        w_main = w_xm_v
    nm = w_main.shape[1]
    tok = lambda i: (i, 0)
    const = lambda i: (0, 0)
    in_specs = [pl.BlockSpec((tm, d), tok),
                pl.BlockSpec((1, 6, d), lambda i: (mod_row(i), 0, 0)),
                pl.BlockSpec((1, d), const),
                pl.BlockSpec((d, nm), const),
                pl.BlockSpec((N_GATES, d), const),
                pl.BlockSpec((N_GATES, 1), const)]
    args = [x2d, mod, norm_w.reshape(1, d), w_main, w_gt, b_gate.reshape(N_GATES, 1)]
    tokb = pl.BlockSpec((tm, M_W), tok)
    out_specs = [tokb, tokb]
    out_shape = [jax.ShapeDtypeStruct((n, M_W), BF16)] * 2
    if fourier:
        in_specs.append(pl.BlockSpec((d, F_W), const))
        args.append(w_in_b[:, :F_W])
        out_specs.append(tokb)
        out_shape.append(jax.ShapeDtypeStruct((n, M_W), BF16))
    out_specs.append(pl.BlockSpec((1, N_GATES, tm), lambda i: (i // tiles_per_seq, 0, i % tiles_per_seq)))
    out_shape.append(jax.ShapeDtypeStruct((nb, N_GATES, seq_len), F32))
    scratch = []
    if fourier:
        sub = tm // DFT_RADIX
        out_specs.append(pl.BlockSpec((1, DFT_RADIX, sub, F_W),
                                      lambda i: (i // tiles_per_seq, 0, i % tiles_per_seq, 0)))
        out_shape.append(jax.ShapeDtypeStruct((nb, DFT_RADIX, seq_len // DFT_RADIX, F_W), BF16))
        scratch.append(pltpu.VMEM((d // 128, tm, 128), F32))
    return pl.pallas_call(
        functools.partial(_inproj_kernel, tm=tm, fourier=fourier),
        grid=(n // tm,),
        in_specs=in_specs, out_specs=out_specs, out_shape=out_shape,
        scratch_shapes=scratch,
        compiler_params=_params(("parallel",)),
        name="inproj_latent" if fourier else "inproj_ctx",
    )(*args)


def _dft_tables(seq_len):
    sub = seq_len // DFT_RADIX
    k2 = np.arange(sub, dtype=np.int64)[None, :, None]
    n1 = np.arange(DFT_RADIX, dtype=np.int64)[:, None, None]
    n2 = np.arange(sub, dtype=np.int64)[None, None, :]
    ang = 2.0 * np.pi * ((k2 * (n1 + DFT_RADIX * n2)) % seq_len).astype(np.float64) / seq_len
    scale = 1.0 / math.sqrt(seq_len)
    return (jnp.asarray(np.cos(ang) * scale, dtype=F32).astype(BF16),
            jnp.asarray(np.sin(ang) * scale, dtype=F32).astype(BF16))


def _cadd(a, b):
    return (a[0] + b[0], a[1] + b[1])


def _csub(a, b):
    return (a[0] - b[0], a[1] - b[1])


def _cmul_w8(z, k):
    re, im = z
    r = math.sqrt(0.5)
    if k == 0:
        return z
    if k == 1:
        return ((re + im) * r, (im - re) * r)
    if k == 2:
        return (im, -re)
    return ((im - re) * r, (-re - im) * r)


def _dft8(z):
    def dft4(a, b, c, d):
        s0, s1 = _cadd(a, c), _csub(a, c)
        t0, t1 = _cadd(b, d), _csub(b, d)
        t1r = _cmul_w8(t1, 2)
        return [_cadd(s0, t0), _cadd(s1, t1r), _csub(s0, t0), _csub(s1, t1r)]
    ev = dft4(z[0], z[2], z[4], z[6])
    od = dft4(z[1], z[3], z[5], z[7])
    out = [None] * 8
    for k in range(4):
        t = _cmul_w8(od[k], k)
        out[k] = _cadd(ev[k], t)
        out[k + 4] = _csub(ev[k], t)
    return out


def _fourier_kernel(u_ref, tcs_ref, p_ref, q_ref, ga_ref, gb_ref, *, sub, rb):
    for n1 in range(DFT_RADIX):
        g = _dot(tcs_ref[n1], u_ref[0, n1])
        ga_ref[n1] = g[:sub]
        gb_ref[n1] = g[sub:]

    def body(i, carry):
        r0 = pl.multiple_of(i * rb, rb)
        z = [(ga_ref[n1, pl.ds(r0, rb), :], -gb_ref[n1, pl.ds(r0, rb), :]) for n1 in range(DFT_RADIX)]
        x = _dft8(z)
        for k1 in range(DFT_RADIX):
            p_ref[pl.ds(k1 * sub + r0, rb), :] = x[k1][0].astype(BF16)
            q_ref[pl.ds(k1 * sub + r0, rb), :] = (-x[k1][1]).astype(BF16)
        return carry

    lax.fori_loop(0, sub // rb, body, 0)


def _fourier(u_perm, seq_len):
    nb = u_perm.shape[0]
    sub = seq_len // DFT_RADIX
    cb = 256
    tcs = jnp.concatenate(_dft_tables(seq_len), axis=1)
    return pl.pallas_call(
        functools.partial(_fourier_kernel, sub=sub, rb=32),
        grid=(nb, F_W // cb),
        in_specs=[pl.BlockSpec((1, DFT_RADIX, sub, cb), lambda b, j: (b, 0, 0, j)),
                  pl.BlockSpec((DFT_RADIX, 2 * sub, sub), lambda b, j: (0, 0, 0))],
        out_specs=[pl.BlockSpec((seq_len, cb), lambda b, j: (b, j))] * 2,
        out_shape=[jax.ShapeDtypeStruct((nb * seq_len, F_W), BF16)] * 2,
        scratch_shapes=[pltpu.VMEM((DFT_RADIX, sub, cb), F32)] * 2,
        compiler_params=_params(("parallel", "parallel")),
        name="fourier",
    )(u_perm, tcs)


def _prep_kernel(xm_ref, prev_ref, next_ref, cw_ref, cb_ref, wq_ref, wkt_ref, a_ref, q_ref, kt_ref,
                 *, t, tiles_per_seq):
    i = pl.program_id(0)
    first = (i % tiles_per_seq) == 0
    last = (i % tiles_per_seq) == tiles_per_seq - 1
    x = xm_ref[...].astype(F32)
    prev_row = jnp.where(first, 0.0, prev_ref[15:16, :].astype(F32))
    next_row = jnp.where(last, 0.0, next_ref[0:1, :].astype(F32))
    rows = lax.broadcasted_iota(jnp.int32, (t, 1), 0)
    xp = jnp.where(rows == 0, prev_row, pltpu.roll(x, 1, 0))
    xn = jnp.where(rows == t - 1, next_row, pltpu.roll(x, t - 1, 0))
    cw = cw_ref[...]
    a = _silu(cw[0:1, :] * xp + cw[1:2, :] * x + cw[2:3, :] * xn + cb_ref[...])
    ab = a.astype(BF16)
    a_ref[...] = ab
    for pair in range(2):
        sl = slice(pair * 256, (pair + 1) * 256)
        q_ref[:, sl] = _dot(ab[:, sl], wq_ref[pair]).astype(BF16)
        kt = lax.dot_general(wkt_ref[pair], ab[:, sl], (((1,), (1,)), ((), ())),
                             preferred_element_type=F32)
        kt_ref[0, sl, :] = (kt * M_HEAD_DIM ** -0.5).astype(BF16)


def _pair_blockdiag(w):
    z = jnp.zeros_like(w[0])
    return jnp.stack([jnp.block([[w[0], z], [z, w[1]]]), jnp.block([[w[2], z], [z, w[3]]])])


def _prep(xm, mconv_w, mconv_b, w_q, w_k, seq_len, t):
    n = xm.shape[0]
    nb = n // seq_len
    tiles_per_seq = seq_len // t
    hb = t // 16
    nhalo = n // 16
    wq_bd = _pair_blockdiag(w_q.astype(BF16))
    wkt_bd = _pair_blockdiag(jnp.swapaxes(w_k, 1, 2).astype(BF16))
    return pl.pallas_call(
        functools.partial(_prep_kernel, t=t, tiles_per_seq=tiles_per_seq),
        grid=(n // t,),
        in_specs=[pl.BlockSpec((t, M_W), lambda i: (i, 0)),
                  pl.BlockSpec((16, M_W), lambda i: (jnp.maximum(i * hb - 1, 0), 0)),
                  pl.BlockSpec((16, M_W), lambda i: (jnp.minimum((i + 1) * hb, nhalo - 1), 0)),
                  pl.BlockSpec((3, M_W), lambda i: (0, 0)),
                  pl.BlockSpec((1, M_W), lambda i: (0, 0)),
                  pl.BlockSpec((2, 256, 256), lambda i: (0, 0, 0)),
                  pl.BlockSpec((2, 256, 256), lambda i: (0, 0, 0))],
        out_specs=[pl.BlockSpec((t, M_W), lambda i: (i, 0)),
                   pl.BlockSpec((t, M_W), lambda i: (i, 0)),
                   pl.BlockSpec((1, M_W, t), lambda i: (i // tiles_per_seq, 0, i % tiles_per_seq))],
        out_shape=[jax.ShapeDtypeStruct((n, M_W), BF16),
                   jax.ShapeDtypeStruct((n, M_W), BF16),
                   jax.ShapeDtypeStruct((nb, M_W, seq_len), BF16)],
        compiler_params=_params(("parallel",)),
        name="prep_%d" % seq_len,
    )(xm, xm, xm, mconv_w, mconv_b.reshape(1, M_W), wq_bd, wkt_bd)


def _log_sigmoid(x):
    return jnp.minimum(x, 0.0) - jnp.log1p(jnp.exp(-jnp.abs(x)))


def _tri(reverse):
    r = lax.broadcasted_iota(jnp.int32, (CHUNK, CHUNK), 0)
    c = lax.broadcasted_iota(jnp.int32, (CHUNK, CHUNK), 1)
    return (c >= r) if reverse else (c <= r)


def _cumsum_rows(gt, reverse):
    t_row = _tri(not reverse).astype(BF16)
    lf = _log_sigmoid(gt) * LOG2_E
    l0, l1, l2 = _split3(lf)
    return lf, _dot(l0, t_row) + _dot(l1, t_row) + _dot(l2, t_row)


def _twice(x):
    return jnp.concatenate([x, x], axis=-1)


def _pair_step(q2, kts, vaugs, causal, lf_rows, brows, li_rows, c_augs, ms, with_output):
    kws, m_news, w_olds, scales, m_ts, w_inters = [], [], [], [], [], []
    for e in range(2):
        rowterm = li_rows[e] - brows[e]
        b_end = jnp.sum(lf_rows[e], axis=-1, keepdims=True)
        g = b_end + rowterm
        m_new = jnp.maximum(b_end + ms[e], jnp.max(g, axis=-1, keepdims=True))
        w_olds.append(jnp.exp2(b_end + ms[e] - m_new))
        kws.append((kts[e].astype(F32) * jnp.exp2(g - m_new)).astype(BF16))
        m_news.append(m_new)
        if with_output:
            bcol = jnp.sum(jnp.where(causal, lf_rows[e], 0.0), axis=-1, keepdims=True)
            dmat = jnp.where(causal, bcol + rowterm, -jnp.inf)
            inter = bcol + ms[e]
            m_t = jnp.maximum(inter, jnp.max(dmat, axis=-1, keepdims=True))
            w_inters.append(jnp.exp2(inter - m_t))
            scales.append(jnp.exp2(dmat - m_t))
            m_ts.append(m_t)
    zk = jnp.zeros((CHUNK, CHUNK), BF16)
    vv = jnp.concatenate(vaugs, axis=0)
    if not with_output:
        sv = _dot(jnp.concatenate([jnp.concatenate([kws[0], zk], axis=-1),
                                   jnp.concatenate([zk, kws[1]], axis=-1)], axis=0), vv)
        return [None, None], [_twice(w_olds[e]) * c_augs[e] + sv[e * CHUNK:(e + 1) * CHUNK] for e in range(2)], m_news
    zc = jnp.zeros((M_HEAD_DIM, 3 * M_HEAD_DIM), BF16)
    rhs = jnp.concatenate(
        [jnp.concatenate([kts[0], c_augs[0].astype(BF16), zc], axis=-1),
         jnp.concatenate([zc, kts[1], c_augs[1].astype(BF16)], axis=-1)], axis=0)
    qkc = _dot(q2, rhs)
    lhs = []
    for e in range(2):
        s = (qkc[:, 3 * CHUNK * e:3 * CHUNK * e + CHUNK] * scales[e]).astype(BF16)
        lhs += [jnp.concatenate([s, zk] if e == 0 else [zk, s], axis=-1),
                jnp.concatenate([kws[e], zk] if e == 0 else [zk, kws[e]], axis=-1)]
    sv = _dot(jnp.concatenate(lhs, axis=0), vv)
    hs, c_news = [], []
    for e in range(2):
        nd = _twice(w_inters[e]) * qkc[:, 3 * CHUNK * e + CHUNK:3 * CHUNK * (e + 1)] + sv[2 * e * CHUNK:(2 * e + 1) * CHUNK]
        hs.append(nd[:, :M_HEAD_DIM] / jnp.maximum(jnp.abs(nd[:, M_HEAD_DIM:]), jnp.exp2(-m_ts[e])))
        c_news.append(_twice(w_olds[e]) * c_augs[e] + sv[(2 * e + 1) * CHUNK:(2 * e + 2) * CHUNK])
    return hs, c_news, m_news


def _scan_kernel(qf_ref, ktf_ref, vf_ref, gtf_ref, qb_ref, ktb_ref, vb_ref, gtb_ref,
                 ktc_ref, vc_ref, gtc_ref, hf_ref, hb_ref, c_scr, m_scr, *, ctx_chunks, nbb):
    c = pl.program_id(1)
    ones = jnp.ones((CHUNK, M_HEAD_DIM), BF16)

    def run_chunk(bb, q, kt, v, gt, d, with_output):
        reverse = d == 1
        causal = _tri(reverse)
        lf, brow = _cumsum_rows(gt, reverse)
        outs = []
        for pair in range(M_HEADS // 2):
            heads = (2 * pair, 2 * pair + 1)
            sls = [slice(hd * M_HEAD_DIM, (hd + 1) * M_HEAD_DIM) for hd in heads]
            gis = [d * 8 + hd for hd in heads]
            gfs = [d * 8 + 4 + hd for hd in heads]
            rs = [(bb * 2 + d) * M_HEADS + hd for hd in heads]
            hs, c_news, m_news = _pair_step(
                q[:, sls[0].start:sls[1].stop] if with_output else None,
                [kt[sl, :] for sl in sls], [jnp.concatenate([v[:, sl], ones], axis=-1) for sl in sls], causal,
                [lf[gf:gf + 1, :] for gf in gfs], [brow[gf:gf + 1, :] for gf in gfs],
                [gt[gi:gi + 1, :] * LOG2_E for gi in gis], [c_scr[r] for r in rs], [m_scr[r, 0:1, :] for r in rs],
                with_output)
            for e, r in enumerate(rs):
                c_scr[r] = c_news[e]
                m_scr[r] = jnp.broadcast_to(m_news[e], (8, 128))
            outs += hs
        return outs

    @pl.when(c == 0)
    def _():
        c_scr[...] = jnp.zeros_like(c_scr)
        m_scr[...] = jnp.zeros_like(m_scr)
        for bb in range(nbb):
            for d in range(2):
                order = range(ctx_chunks) if d == 0 else range(ctx_chunks - 1, -1, -1)
                for j in order:
                    rs = slice(j * CHUNK, (j + 1) * CHUNK)
                    run_chunk(bb, None, ktc_ref[bb, :, rs], vc_ref[bb, rs, :], gtc_ref[bb, :, rs], d, False)

    for bb in range(nbb):
        hf = run_chunk(bb, qf_ref[bb], ktf_ref[bb], vf_ref[bb], gtf_ref[bb], 0, True)
        hb = run_chunk(bb, qb_ref[bb], ktb_ref[bb], vb_ref[bb], gtb_ref[bb], 1, True)
        for hd in range(M_HEADS):
            sl = slice(hd * M_HEAD_DIM, (hd + 1) * M_HEAD_DIM)
            hf_ref[bb, :, sl] = hf[hd].astype(BF16)
            hb_ref[bb, :, sl] = hb[hd].astype(BF16)


def _scan(q, kt, v, gt, kt_c, v_c, gt_c, seq_len, ctx_len, nbb):
    n = q.shape[0]
    nb = n // seq_len
    nc = seq_len // CHUNK
    q3, v3 = q.reshape(nb, seq_len, M_W), v.reshape(nb, seq_len, M_W)
    fwd = lambda b, c: (b, c, 0)
    bwd = lambda b, c: (b, nc - 1 - c, 0)
    fwd_t = lambda b, c: (b, 0, c)
    bwd_t = lambda b, c: (b, 0, nc - 1 - c)
    tokb = lambda im: pl.BlockSpec((nbb, CHUNK, M_W), im)

    def side(tok_map, tok_map_t):
        return [tokb(tok_map), pl.BlockSpec((nbb, M_W, CHUNK), tok_map_t), tokb(tok_map),
                pl.BlockSpec((nbb, N_GATES, CHUNK), tok_map_t)]

    in_specs = (side(fwd, fwd_t) + side(bwd, bwd_t)
                + [pl.BlockSpec((nbb, M_W, ctx_len), lambda b, c: (b, 0, 0)),
                   pl.BlockSpec((nbb, ctx_len, M_W), lambda b, c: (b, 0, 0)),
                   pl.BlockSpec((nbb, N_GATES, ctx_len), lambda b, c: (b, 0, 0))])
    hf, hb = pl.pallas_call(
        functools.partial(_scan_kernel, ctx_chunks=ctx_len // CHUNK, nbb=nbb),
        grid=(nb // nbb, nc),
        in_specs=in_specs,
        out_specs=[tokb(fwd), tokb(bwd)],
        out_shape=[jax.ShapeDtypeStruct((nb, seq_len, M_W), BF16)] * 2,
        scratch_shapes=[pltpu.VMEM((nbb * 2 * M_HEADS, M_HEAD_DIM, 2 * M_HEAD_DIM), F32),
                        pltpu.VMEM((nbb * 2 * M_HEADS, 8, 128), F32)],
        compiler_params=_params(("parallel", "arbitrary")),
        name="mlstm_scan",
    )(q3, kt, v3, gt, q3, kt, v3, gt, kt_c, v_c.reshape(nb, ctx_len, M_W), gt_c)
    return hf.reshape(n, M_W), hb.reshape(n, M_W)


def _mixout_kernel(x_ref, mod_ref, p_ref, q_ref, hf_ref, hb_ref, a_ref, z_ref, mnw_ref, msk_ref,
                   wc_ref, wo_ref, n2w_ref, x1_ref, h2_ref):
    mod = mod_ref[0]
    gate1, shift2, scale2 = mod[2:3, :], mod[3:4, :], mod[4:5, :]
    yf = _dot(p_ref[...], wc_ref[0]) + _dot(q_ref[...], wc_ref[1])
    h = hf_ref[...].astype(F32) + hb_ref[...].astype(F32)
    parts = []
    for hd in range(M_HEADS):
        hh = h[:, hd * M_HEAD_DIM:(hd + 1) * M_HEAD_DIM]
        mu = jnp.mean(hh, axis=-1, keepdims=True)
        dlt = hh - mu
        var = jnp.mean(dlt * dlt, axis=-1, keepdims=True)
        parts.append(dlt * lax.rsqrt(var + NORM_EPS))
    hn = jnp.concatenate(parts, axis=-1) * mnw_ref[...]
    ym = (hn + msk_ref[...] * a_ref[...].astype(F32)) * _silu(z_ref[...].astype(F32))
    y = _dot(yf.astype(BF16), wo_ref[0]) + _dot(ym.astype(BF16), wo_ref[1])
    x1 = x_ref[...] + gate1 * y
    x1_ref[...] = x1
    h2_ref[...] = (_rms(x1, n2w_ref[...]) * (1.0 + scale2) + shift2).astype(BF16)


def _channel_dft():
    k = np.arange(F_GROUP_W, dtype=np.int64)
    ang = 2.0 * np.pi * ((k[:, None] * k[None, :]) % F_GROUP_W).astype(np.float64) / F_GROUP_W
    scale = 1.0 / math.sqrt(F_GROUP_W)
    eye = np.eye(F_GROUPS)
    wc = np.stack([np.kron(eye, np.cos(ang) * scale), -np.kron(eye, np.sin(ang) * scale)])
    return jnp.asarray(wc, dtype=F32).astype(BF16)


def _mixout(x2d, mod, p, q, hf, hb, a, z, mnorm_w, m_skip, w_out, norm2_w, seq_len, tm):
    n, d = x2d.shape
    tiles_per_seq = seq_len // tm
    tok = lambda i: (i, 0)
    half = pl.BlockSpec((tm, F_W), tok)
    full = pl.BlockSpec((tm, d), tok)
    vec = lambda w: pl.BlockSpec((1, w), lambda i: (0, 0))
    return pl.pallas_call(
        _mixout_kernel,
        grid=(n // tm,),
        in_specs=[full, pl.BlockSpec((1, 6, d), lambda i: (i // tiles_per_seq, 0, 0)),
                  half, half, half, half, half, half, vec(M_W), vec(M_W),
                  pl.BlockSpec((2, F_W, F_W), lambda i: (0, 0, 0)),
                  pl.BlockSpec((2, F_W, d), lambda i: (0, 0, 0)),
                  vec(d)],
        out_specs=[full, full],
        out_shape=[jax.ShapeDtypeStruct((n, d), F32), jax.ShapeDtypeStruct((n, d), BF16)],
        compiler_params=_params(("parallel",)),
        name="mixout",
    )(x2d, mod, p, q, hf, hb, a, z, mnorm_w.reshape(1, M_W), m_skip.reshape(1, M_W),
      _channel_dft(), w_out.astype(BF16).reshape(2, F_W, d), norm2_w.reshape(1, d))


def _ffn_kernel(h_ref, hp_ref, hn_ref, x1_ref, mod_ref, wu_ref, bu_ref, cw_ref, cb_ref, wd_ref, bd_ref,
                fw_ref, o_ref, hcat_scr, ua_scr, ub_scr, act_scr, *, rows, tiles, cw):
    t = pl.program_id(1)
    nt = rows * GRID_W
    nj = D_FF // cw
    hcat_scr[0:GRID_W] = hp_ref[...]
    hcat_scr[GRID_W:GRID_W + nt] = h_ref[...]
    hcat_scr[GRID_W + nt:] = hn_ref[...]
    row8 = lax.broadcasted_iota(jnp.int32, (8, 1), 0)
    o_ref[...] = jnp.zeros_like(o_ref)

    def up(j, u_scr):
        u = _dot(hcat_scr[...], wu_ref[j]) + bu_ref[j]
        for p in range(rows + 2):
            rs = slice(p * GRID_W, (p + 1) * GRID_W)
            piece = u[rs]
            if p == 0:
                piece = jnp.where(t > 0, piece, 0.0)
            if p == rows + 1:
                piece = jnp.where(t < tiles - 1, piece, 0.0)
            prev = pltpu.roll(piece, 1, 0)
            prev = jnp.concatenate([jnp.where(row8 == 0, 0.0, prev[0:8]), prev[8:]], axis=0)
            nxt = pltpu.roll(piece, GRID_W - 1, 0)
            nxt = jnp.concatenate([nxt[:GRID_W - 8], jnp.where(row8 == 7, 0.0, nxt[GRID_W - 8:])], axis=0)
            u_scr[0, rs] = prev.astype(CONV_DTYPE)
            u_scr[1, rs] = piece.astype(CONV_DTYPE)
            u_scr[2, rs] = nxt.astype(CONV_DTYPE)

    def consume(j, u_scr):
        w = cw_ref[j]
        for r in range(rows):
            conv = cb_ref[j]
            for dr in range(3):
                rs = slice((r + dr) * GRID_W, (r + dr + 1) * GRID_W)
                for dc in range(3):
                    conv = conv + w[3 * dr + dc:3 * dr + dc + 1, :] * u_scr[dc, rs]
            act_scr[r * GRID_W:(r + 1) * GRID_W, :] = (conv[:, :cw] * _silu(conv[:, cw:])).astype(BF16)
        o_ref[...] += _dot(act_scr[...], wd_ref[j])

    up(0, ua_scr)

    def body(jj, carry):
        j = 2 * jj
        up(j + 1, ub_scr)
        consume(j, ua_scr)
        up(j + 2, ua_scr)
        consume(j + 1, ub_scr)
        return carry

    for jj in range(nj // 2 - 1):
        body(jj, 0)
    up(nj - 1, ub_scr)
    consume(nj - 2, ua_scr)
    consume(nj - 1, ub_scr)
    y = o_ref[...] + bd_ref[...]
    o_ref[...] = _rms(x1_ref[...] + mod_ref[0][5:6, :] * y, fw_ref[...])


def _pair_cast_kernel(val_ref, gate_ref, o_ref, *, cw):
    o_ref[0, :, :cw] = val_ref[...].astype(BF16)
    o_ref[0, :, cw:] = gate_ref[...].astype(BF16)


def _pair_cast(w_up, cw):
    d = w_up.shape[0]
    nj = D_FF // cw
    return pl.pallas_call(
        functools.partial(_pair_cast_kernel, cw=cw),
        grid=(nj,),
        in_specs=[pl.BlockSpec((d, cw), lambda j: (0, j)), pl.BlockSpec((d, cw), lambda j: (0, nj + j))],
        out_specs=pl.BlockSpec((1, d, 2 * cw), lambda j: (j, 0, 0)),
        out_shape=jax.ShapeDtypeStruct((nj, d, 2 * cw), BF16),
        compiler_params=_params(("parallel",)),
        name="pair_cast",
    )(w_up, w_up)


def _ffn(h2, x1, mod, w_up, b_up, fconv_w, fconv_b, w_down, b_down, final_norm_w, seq_len, rows, cw):
    n, d = x1.shape
    nb = n // seq_len
    grid_h = seq_len // GRID_W
    tiles = grid_h // rows
    nt = rows * GRID_W
    nj = D_FF // cw

    def pair(w):
        lead = w.shape[:-1]
        w2 = w.reshape(lead + (2, nj, cw))
        w2 = jnp.moveaxis(w2, -2, 0)
        return w2.reshape((nj,) + lead + (2 * cw,))

    wu = _pair_cast(w_up, cw)
    bu = pair(b_up.reshape(1, 2 * D_FF))
    cwt = pair(fconv_w.reshape(9, 2 * D_FF)).astype(CONV_DTYPE)
    cbt = pair(fconv_b.reshape(1, 2 * D_FF)).astype(CONV_DTYPE)
    wd = w_down.astype(BF16).reshape(nj, cw, d)
    nblk = n // GRID_W
    const3 = lambda b, t: (0, 0, 0)
    return pl.pallas_call(
        functools.partial(_ffn_kernel, rows=rows, tiles=tiles, cw=cw),
        grid=(nb, tiles),
        in_specs=[pl.BlockSpec((nt, d), lambda b, t: (b * tiles + t, 0)),
                  pl.BlockSpec((GRID_W, d), lambda b, t: (jnp.maximum((b * tiles + t) * rows - 1, 0), 0)),
                  pl.BlockSpec((GRID_W, d), lambda b, t: (jnp.minimum((b * tiles + t + 1) * rows, nblk - 1), 0)),
                  pl.BlockSpec((nt, d), lambda b, t: (b * tiles + t, 0)),
                  pl.BlockSpec((1, 6, d), lambda b, t: (b, 0, 0)),
                  pl.BlockSpec((nj, d, 2 * cw), const3, pipeline_mode=pl.Buffered(1)),
                  pl.BlockSpec((nj, 1, 2 * cw), const3),
                  pl.BlockSpec((nj, 9, 2 * cw), const3),
                  pl.BlockSpec((nj, 1, 2 * cw), const3),
                  pl.BlockSpec((nj, cw, d), const3, pipeline_mode=pl.Buffered(1)),
                  pl.BlockSpec((1, d), lambda b, t: (0, 0)),
                  pl.BlockSpec((1, d), lambda b, t: (0, 0))],
        out_specs=pl.BlockSpec((nt, d), lambda b, t: (b * tiles + t, 0)),
        out_shape=jax.ShapeDtypeStruct((n, d), F32),
        scratch_shapes=[pltpu.VMEM((nt + 2 * GRID_W, d), BF16),
                        pltpu.VMEM((3, nt + 2 * GRID_W, 2 * cw), CONV_DTYPE),
                        pltpu.VMEM((3, nt + 2 * GRID_W, 2 * cw), CONV_DTYPE),
                        pltpu.VMEM((nt, cw), BF16)],
        compiler_params=_params(("parallel", "parallel")),
        name="conv_ffn",
    )(h2, h2, h2, x1, mod, wu, bu, cwt, cbt, wd, b_down.reshape(1, d), final_norm_w.reshape(1, d))


def kernel(x, c, ctx, c_ctx, w_ada, b_ada, norm1_w, w_in, mconv_w, mconv_b, w_q, w_k, b_gate, mnorm_w, m_skip,
           w_out, norm2_w, w_up, b_up, fconv_w, fconv_b, w_down, b_down, final_norm_w):
    bsz, seq_len, d = x.shape
    ctx_len = ctx.shape[1]
    assert w_ada.shape[0] == 1, "single-layer kernel"
    cond = jnp.concatenate([c, c_ctx[None, :], jnp.zeros((16 - bsz - 1, d), F32)], axis=0)
    mod = _adaln(cond, w_ada[0], b_ada[0]).reshape(16, 6, d)

    x2d = x.reshape(bsz * seq_len, d)
    ctx2d = ctx.reshape(bsz * ctx_len, d)

    xm_c, v_c, gt_c = _inproj(ctx2d, mod, lambda i: bsz, norm1_w[0], w_in[0], b_gate[0],
                              ctx_len, ctx_len, False)
    _, _, kt_c = _prep(xm_c, mconv_w[0], mconv_b[0], w_q[0], w_k[0], ctx_len, ctx_len)

    xm, v, z, gt, u_perm = _inproj(x2d, mod, lambda i: i // (seq_len // INPROJ_TILE), norm1_w[0], w_in[0],
                                   b_gate[0], seq_len, INPROJ_TILE, True)
    p, q_im = _fourier(u_perm, seq_len)
    a, q, kt = _prep(xm, mconv_w[0], mconv_b[0], w_q[0], w_k[0], seq_len, PREP_TILE)
    hf, hb = _scan(q, kt, v, gt, kt_c, v_c, gt_c, seq_len, ctx_len, SCAN_SEQS)
    x1, h2 = _mixout(x2d, mod, p, q_im, hf, hb, a, z, mnorm_w[0], m_skip[0], w_out[0], norm2_w[0],
                     seq_len, MIXOUT_TILE)
    out = _ffn(h2, x1, mod, w_up[0], b_up[0], fconv_w[0], fconv_b[0], w_down[0], b_down[0],
               final_norm_w, seq_len, FFN_GRID_ROWS, FFN_CHANNELS)
    return out.reshape(bsz, seq_len, d)
```

```python
import functools
import math

import numpy as np
import jax
import jax.numpy as jnp
from jax import lax
from jax.experimental import pallas as pl
from jax.experimental.pallas import tpu as pltpu

F32 = jnp.float32
BF16 = jnp.bfloat16

D_MODEL = 1024
GRID_W = 64
F_W = 512
F_GROUPS = 4
F_GROUP_W = F_W // F_GROUPS
M_W = 512
M_HEADS = 4
M_HEAD_DIM = M_W // M_HEADS
N_GATES = 16
MLSTM_END = F_W + 2 * M_W + N_GATES
CHUNK = 128
D_FF = 5 * D_MODEL // 2
NORM_EPS = 1e-6
LOG2_E = 1.4426950408889634

DFT_RADIX = 8
VMEM_LIMIT = 56 * 1024 * 1024
INPROJ_TILE = 1024
MIXOUT_TILE = 1024
PREP_TILE = 2048
SCAN_SEQS = 8
FFN_GRID_ROWS = 16
FFN_CHANNELS = 256
CONV_DTYPE = BF16
CONV_ROWS = 16


def _params(sem, vmem=VMEM_LIMIT):
    return pltpu.CompilerParams(dimension_semantics=sem, vmem_limit_bytes=vmem)


def _split3(a):
    hi = a.astype(BF16)
    r1 = a - hi.astype(F32)
    mid = r1.astype(BF16)
    lo = (r1 - mid.astype(F32)).astype(BF16)
    return hi, mid, lo


def _dot(a, b):
    return jnp.dot(a, b, preferred_element_type=F32)


def _dot_f32(a, b):
    a0, a1, a2 = _split3(a)
    b0, b1, b2 = _split3(b)
    return (_dot(a0, b0) + (_dot(a0, b1) + _dot(a1, b0))
            + (_dot(a1, b1) + _dot(a0, b2) + _dot(a2, b0)))


def _silu(x):
    return x * jax.nn.sigmoid(x)


def _rms(x, w):
    return x * lax.rsqrt(jnp.mean(x * x, axis=-1, keepdims=True) + NORM_EPS) * w


def _adaln_kernel(cond_ref, w_ref, b_ref, o_ref):
    s = _silu(cond_ref[...])
    o_ref[...] = _dot_f32(s, w_ref[...]) + b_ref[...]


def _adaln(cond, w_ada, b_ada):
    rows, d = cond.shape
    n = w_ada.shape[1]
    bn = 1536
    return pl.pallas_call(
        _adaln_kernel,
        grid=(n // bn,),
        in_specs=[pl.BlockSpec((rows, d), lambda j: (0, 0)),
                  pl.BlockSpec((d, bn), lambda j: (0, j)),
                  pl.BlockSpec((1, bn), lambda j: (0, j))],
        out_specs=pl.BlockSpec((rows, bn), lambda j: (0, j)),
        out_shape=jax.ShapeDtypeStruct((rows, n), F32),
        compiler_params=_params(("parallel",)),
        name="adaln",
    )(cond, w_ada, b_ada.reshape(1, n))


def _inproj_kernel(x_ref, mod_ref, nw_ref, wm_ref, wgt_ref, bgt_ref, *rest, tm, fourier):
    if fourier:
        wf_ref, xm_ref, v_ref, z_ref, gt_ref, u_ref, h_scr = rest
    else:
        xm_ref, v_ref, gt_ref = rest
    mod = mod_ref[0]
    shift, scale = mod[0:1, :], mod[1:2, :]
    h = _rms(x_ref[...], nw_ref[...]) * (1.0 + scale) + shift
    hb = h.astype(BF16)
    p = _dot(hb, wm_ref[...])
    xm_ref[...] = p[:, 0:M_W].astype(BF16)
    v_ref[...] = p[:, M_W:2 * M_W].astype(BF16)
    gt_ref[0] = lax.dot_general(wgt_ref[...], hb, (((1,), (1,)), ((), ())),
                                preferred_element_type=F32) + bgt_ref[...]
    if fourier:
        z_ref[...] = p[:, 2 * M_W:3 * M_W].astype(BF16)
        nlb = h.shape[1] // 128
        for lb in range(nlb):
            h_scr[lb] = h[:, lb * 128:(lb + 1) * 128]
        sub = tm // DFT_RADIX
        hs = jnp.concatenate(
            [jnp.concatenate([h_scr[lb, pl.ds(n1, sub, stride=DFT_RADIX), :] for lb in range(nlb)], axis=1)
             for n1 in range(DFT_RADIX)], axis=0)
        uf = _dot(hs.astype(BF16), wf_ref[...])
        for n1 in range(DFT_RADIX):
            u_ref[0, n1] = uf[n1 * sub:(n1 + 1) * sub, :].astype(BF16)


def _inproj(x2d, mod, mod_row, norm_w, w_in, b_gate, seq_len, tm, fourier):
    n, d = x2d.shape
    tiles_per_seq = seq_len // tm
    nb = n // seq_len
    w_in_b = w_in.astype(BF16)
    w_xm_v = w_in_b[:, F_W:F_W + 2 * M_W]
    w_gt = w_in_b[:, F_W + 2 * M_W:MLSTM_END].T
    if fourier:
        w_main = jnp.concatenate([w_xm_v, w_in_b[:, MLSTM_END:]], axis=1)
    else:
        w_main = w_xm_v
    nm = w_main.shape[1]
    tok = lambda i: (i, 0)
    const = lambda i: (0, 0)
    in_specs = [pl.BlockSpec((tm, d), tok),
                pl.BlockSpec((1, 6, d), lambda i: (mod_row(i), 0, 0)),
                pl.BlockSpec((1, d), const),
                pl.BlockSpec((d, nm), const),
                pl.BlockSpec((N_GATES, d), const),
                pl.BlockSpec((N_GATES, 1), const)]
    args = [x2d, mod, norm_w.reshape(1, d), w_main, w_gt, b_gate.reshape(N_GATES, 1)]
    tokb = pl.BlockSpec((tm, M_W), tok)
    out_specs = [tokb, tokb]
    out_shape = [jax.ShapeDtypeStruct((n, M_W), BF16)] * 2
    if fourier:
        in_specs.append(pl.BlockSpec((d, F_W), const))
        args.append(w_in_b[:, :F_W])
        out_specs.append(tokb)
        out_shape.append(jax.ShapeDtypeStruct((n, M_W), BF16))
    out_specs.append(pl.BlockSpec((1, N_GATES, tm), lambda i: (i // tiles_per_seq, 0, i % tiles_per_seq)))
    out_shape.append(jax.ShapeDtypeStruct((nb, N_GATES, seq_len), F32))
    scratch = []
    if fourier:
        sub = tm // DFT_RADIX
        out_specs.append(pl.BlockSpec((1, DFT_RADIX, sub, F_W),
                                      lambda i: (i // tiles_per_seq, 0, i % tiles_per_seq, 0)))
        out_shape.append(jax.ShapeDtypeStruct((nb, DFT_RADIX, seq_len // DFT_RADIX, F_W), BF16))
        scratch.append(pltpu.VMEM((d // 128, tm, 128), F32))
    return pl.pallas_call(
        functools.partial(_inproj_kernel, tm=tm, fourier=fourier),
        grid=(n // tm,),
        in_specs=in_specs, out_specs=out_specs, out_shape=out_shape,
        scratch_shapes=scratch,
        compiler_params=_params(("parallel",)),
        name="inproj_latent" if fourier else "inproj_ctx",
    )(*args)


def _dft_tables(seq_len):
    sub = seq_len // DFT_RADIX
    k2 = np.arange(sub, dtype=np.int64)[None, :, None]
    n1 = np.arange(DFT_RADIX, dtype=np.int64)[:, None, None]
    n2 = np.arange(sub, dtype=np.int64)[None, None, :]
    ang = 2.0 * np.pi * ((k2 * (n1 + DFT_RADIX * n2)) % seq_len).astype(np.float64) / seq_len
    scale = 1.0 / math.sqrt(seq_len)
    return (jnp.asarray(np.cos(ang) * scale, dtype=F32).astype(BF16),
            jnp.asarray(np.sin(ang) * scale, dtype=F32).astype(BF16))


def _cadd(a, b):
    return (a[0] + b[0], a[1] + b[1])


def _csub(a, b):
    return (a[0] - b[0], a[1] - b[1])


def _cmul_w8(z, k):
    re, im = z
    r = math.sqrt(0.5)
    if k == 0:
        return z
    if k == 1:
        return ((re + im) * r, (im - re) * r)
    if k == 2:
        return (im, -re)
    return ((im - re) * r, (-re - im) * r)


def _dft8(z):
    def dft4(a, b, c, d):
        s0, s1 = _cadd(a, c), _csub(a, c)
        t0, t1 = _cadd(b, d), _csub(b, d)
        t1r = _cmul_w8(t1, 2)
        return [_cadd(s0, t0), _cadd(s1, t1r), _csub(s0, t0), _csub(s1, t1r)]
    ev = dft4(z[0], z[2], z[4], z[6])
    od = dft4(z[1], z[3], z[5], z[7])
    out = [None] * 8
    for k in range(4):
        t = _cmul_w8(od[k], k)
        out[k] = _cadd(ev[k], t)
        out[k + 4] = _csub(ev[k], t)
    return out


def _fourier_kernel(u_ref, tcs_ref, p_ref, q_ref, ga_ref, gb_ref, *, sub, rb):
    for n1 in range(DFT_RADIX):
        g = _dot(tcs_ref[n1], u_ref[0, n1])
        ga_ref[n1] = g[:sub]
        gb_ref[n1] = g[sub:]

    def body(i, carry):
        r0 = pl.multiple_of(i * rb, rb)
        z = [(ga_ref[n1, pl.ds(r0, rb), :], -gb_ref[n1, pl.ds(r0, rb), :]) for n1 in range(DFT_RADIX)]
        x = _dft8(z)
        for k1 in range(DFT_RADIX):
            p_ref[pl.ds(k1 * sub + r0, rb), :] = x[k1][0].astype(BF16)
            q_ref[pl.ds(k1 * sub + r0, rb), :] = (-x[k1][1]).astype(BF16)
        return carry

    lax.fori_loop(0, sub // rb, body, 0)


def _fourier(u_perm, seq_len):
    nb = u_perm.shape[0]
    sub = seq_len // DFT_RADIX
    cb = 256
    tcs = jnp.concatenate(_dft_tables(seq_len), axis=1)
    return pl.pallas_call(
        functools.partial(_fourier_kernel, sub=sub, rb=32),
        grid=(nb, F_W // cb),
        in_specs=[pl.BlockSpec((1, DFT_RADIX, sub, cb), lambda b, j: (b, 0, 0, j)),
                  pl.BlockSpec((DFT_RADIX, 2 * sub, sub), lambda b, j: (0, 0, 0))],
        out_specs=[pl.BlockSpec((seq_len, cb), lambda b, j: (b, j))] * 2,
        out_shape=[jax.ShapeDtypeStruct((nb * seq_len, F_W), BF16)] * 2,
        scratch_shapes=[pltpu.VMEM((DFT_RADIX, sub, cb), F32)] * 2,
        compiler_params=_params(("parallel", "parallel")),
        name="fourier",
    )(u_perm, tcs)


def _prep_kernel(xm_ref, prev_ref, next_ref, cw_ref, cb_ref, wq_ref, wkt_ref, a_ref, q_ref, kt_ref,
                 *, t, tiles_per_seq):
    i = pl.program_id(0)
    first = (i % tiles_per_seq) == 0
    last = (i % tiles_per_seq) == tiles_per_seq - 1
    x = xm_ref[...].astype(F32)
    prev_row = jnp.where(first, 0.0, prev_ref[15:16, :].astype(F32))
    next_row = jnp.where(last, 0.0, next_ref[0:1, :].astype(F32))
    rows = lax.broadcasted_iota(jnp.int32, (t, 1), 0)
    xp = jnp.where(rows == 0, prev_row, pltpu.roll(x, 1, 0))
    xn = jnp.where(rows == t - 1, next_row, pltpu.roll(x, t - 1, 0))
    cw = cw_ref[...]
    a = _silu(cw[0:1, :] * xp + cw[1:2, :] * x + cw[2:3, :] * xn + cb_ref[...])
    ab = a.astype(BF16)
    a_ref[...] = ab
    for pair in range(2):
        sl = slice(pair * 256, (pair + 1) * 256)
        q_ref[:, sl] = _dot(ab[:, sl], wq_ref[pair]).astype(BF16)
        kt = lax.dot_general(wkt_ref[pair], ab[:, sl], (((1,), (1,)), ((), ())),
                             preferred_element_type=F32)
        kt_ref[0, sl, :] = (kt * M_HEAD_DIM ** -0.5).astype(BF16)


def _pair_blockdiag(w):
    z = jnp.zeros_like(w[0])
    return jnp.stack([jnp.block([[w[0], z], [z, w[1]]]), jnp.block([[w[2], z], [z, w[3]]])])


def _prep(xm, mconv_w, mconv_b, w_q, w_k, seq_len, t):
    n = xm.shape[0]
    nb = n // seq_len
    tiles_per_seq = seq_len // t
    hb = t // 16
    nhalo = n // 16
    wq_bd = _pair_blockdiag(w_q.astype(BF16))
    wkt_bd = _pair_blockdiag(jnp.swapaxes(w_k, 1, 2).astype(BF16))
    return pl.pallas_call(
        functools.partial(_prep_kernel, t=t, tiles_per_seq=tiles_per_seq),
        grid=(n // t,),
        in_specs=[pl.BlockSpec((t, M_W), lambda i: (i, 0)),
                  pl.BlockSpec((16, M_W), lambda i: (jnp.maximum(i * hb - 1, 0), 0)),
                  pl.BlockSpec((16, M_W), lambda i: (jnp.minimum((i + 1) * hb, nhalo - 1), 0)),
                  pl.BlockSpec((3, M_W), lambda i: (0, 0)),
                  pl.BlockSpec((1, M_W), lambda i: (0, 0)),
                  pl.BlockSpec((2, 256, 256), lambda i: (0, 0, 0)),
                  pl.BlockSpec((2, 256, 256), lambda i: (0, 0, 0))],
        out_specs=[pl.BlockSpec((t, M_W), lambda i: (i, 0)),
                   pl.BlockSpec((t, M_W), lambda i: (i, 0)),
                   pl.BlockSpec((1, M_W, t), lambda i: (i // tiles_per_seq, 0, i % tiles_per_seq))],
        out_shape=[jax.ShapeDtypeStruct((n, M_W), BF16),
                   jax.ShapeDtypeStruct((n, M_W), BF16),
                   jax.ShapeDtypeStruct((nb, M_W, seq_len), BF16)],
        compiler_params=_params(("parallel",)),
        name="prep_%d" % seq_len,
    )(xm, xm, xm, mconv_w, mconv_b.reshape(1, M_W), wq_bd, wkt_bd)


def _log_sigmoid(x):
    return jnp.minimum(x, 0.0) - jnp.log1p(jnp.exp(-jnp.abs(x)))


def _tri(reverse):
    r = lax.broadcasted_iota(jnp.int32, (CHUNK, CHUNK), 0)
    c = lax.broadcasted_iota(jnp.int32, (CHUNK, CHUNK), 1)
    return (c >= r) if reverse else (c <= r)


def _cumsum_rows(gt, reverse):
    t_row = _tri(not reverse).astype(BF16)
    lf = _log_sigmoid(gt) * LOG2_E
    l0, l1, l2 = _split3(lf)
    return lf, _dot(l0, t_row) + _dot(l1, t_row) + _dot(l2, t_row)


def _twice(x):
    return jnp.concatenate([x, x], axis=-1)


def _pair_step(q2, kts, vaugs, causal, lf_rows, brows, li_rows, c_augs, ms, with_output):
    kws, m_news, w_olds, scales, m_ts, w_inters = [], [], [], [], [], []
    for e in range(2):
        rowterm = li_rows[e] - brows[e]
        b_end = jnp.sum(lf_rows[e], axis=-1, keepdims=True)
        g = b_end + rowterm
        m_new = jnp.maximum(b_end + ms[e], jnp.max(g, axis=-1, keepdims=True))
        w_olds.append(jnp.exp2(b_end + ms[e] - m_new))
        kws.append((kts[e].astype(F32) * jnp.exp2(g - m_new)).astype(BF16))
        m_news.append(m_new)
        if with_output:
            bcol = jnp.sum(jnp.where(causal, lf_rows[e], 0.0), axis=-1, keepdims=True)
            dmat = jnp.where(causal, bcol + rowterm, -jnp.inf)
            inter = bcol + ms[e]
            m_t = jnp.maximum(inter, jnp.max(dmat, axis=-1, keepdims=True))
            w_inters.append(jnp.exp2(inter - m_t))
            scales.append(jnp.exp2(dmat - m_t))
            m_ts.append(m_t)
    zk = jnp.zeros((CHUNK, CHUNK), BF16)
    vv = jnp.concatenate(vaugs, axis=0)
    if not with_output:
        sv = _dot(jnp.concatenate([jnp.concatenate([kws[0], zk], axis=-1),
                                   jnp.concatenate([zk, kws[1]], axis=-1)], axis=0), vv)
        return [None, None], [_twice(w_olds[e]) * c_augs[e] + sv[e * CHUNK:(e + 1) * CHUNK] for e in range(2)], m_news
    zc = jnp.zeros((M_HEAD_DIM, 3 * M_HEAD_DIM), BF16)
    rhs = jnp.concatenate(
        [jnp.concatenate([kts[0], c_augs[0].astype(BF16), zc], axis=-1),
         jnp.concatenate([zc, kts[1], c_augs[1].astype(BF16)], axis=-1)], axis=0)
    qkc = _dot(q2, rhs)
    lhs = []
    for e in range(2):
        s = (qkc[:, 3 * CHUNK * e:3 * CHUNK * e + CHUNK] * scales[e]).astype(BF16)
        lhs += [jnp.concatenate([s, zk] if e == 0 else [zk, s], axis=-1),
                jnp.concatenate([kws[e], zk] if e == 0 else [zk, kws[e]], axis=-1)]
    sv = _dot(jnp.concatenate(lhs, axis=0), vv)
    hs, c_news = [], []
    for e in range(2):
        nd = _twice(w_inters[e]) * qkc[:, 3 * CHUNK * e + CHUNK:3 * CHUNK * (e + 1)] + sv[2 * e * CHUNK:(2 * e + 1) * CHUNK]
        hs.append(nd[:, :M_HEAD_DIM] / jnp.maximum(jnp.abs(nd[:, M_HEAD_DIM:]), jnp.exp2(-m_ts[e])))
        c_news.append(_twice(w_olds[e]) * c_augs[e] + sv[(2 * e + 1) * CHUNK:(2 * e + 2) * CHUNK])
    return hs, c_news, m_news


def _scan_kernel(qf_ref, ktf_ref, vf_ref, gtf_ref, qb_ref, ktb_ref, vb_ref, gtb_ref,
                 ktc_ref, vc_ref, gtc_ref, hf_ref, hb_ref, c_scr, m_scr, *, ctx_chunks, nbb):
    c = pl.program_id(1)
    ones = jnp.ones((CHUNK, M_HEAD_DIM), BF16)

    def run_chunk(bb, q, kt, v, gt, d, with_output):
        reverse = d == 1
        causal = _tri(reverse)
        lf, brow = _cumsum_rows(gt, reverse)
        outs = []
        for pair in range(M_HEADS // 2):
            heads = (2 * pair, 2 * pair + 1)
            sls = [slice(hd * M_HEAD_DIM, (hd + 1) * M_HEAD_DIM) for hd in heads]
            gis = [d * 8 + hd for hd in heads]
            gfs = [d * 8 + 4 + hd for hd in heads]
            rs = [(bb * 2 + d) * M_HEADS + hd for hd in heads]
            hs, c_news, m_news = _pair_step(
                q[:, sls[0].start:sls[1].stop] if with_output else None,
                [kt[sl, :] for sl in sls], [jnp.concatenate([v[:, sl], ones], axis=-1) for sl in sls], causal,
                [lf[gf:gf + 1, :] for gf in gfs], [brow[gf:gf + 1, :] for gf in gfs],
                [gt[gi:gi + 1, :] * LOG2_E for gi in gis], [c_scr[r] for r in rs], [m_scr[r, 0:1, :] for r in rs],
                with_output)
            for e, r in enumerate(rs):
                c_scr[r] = c_news[e]
                m_scr[r] = jnp.broadcast_to(m_news[e], (8, 128))
            outs += hs
        return outs

    @pl.when(c == 0)
    def _():
        c_scr[...] = jnp.zeros_like(c_scr)
        m_scr[...] = jnp.zeros_like(m_scr)
        for bb in range(nbb):
            for d in range(2):
                order = range(ctx_chunks) if d == 0 else range(ctx_chunks - 1, -1, -1)
                for j in order:
                    rs = slice(j * CHUNK, (j + 1) * CHUNK)
                    run_chunk(bb, None, ktc_ref[bb, :, rs], vc_ref[bb, rs, :], gtc_ref[bb, :, rs], d, False)

    for bb in range(nbb):
        hf = run_chunk(bb, qf_ref[bb], ktf_ref[bb], vf_ref[bb], gtf_ref[bb], 0, True)
        hb = run_chunk(bb, qb_ref[bb], ktb_ref[bb], vb_ref[bb], gtb_ref[bb], 1, True)
        for hd in range(M_HEADS):
            sl = slice(hd * M_HEAD_DIM, (hd + 1) * M_HEAD_DIM)
            hf_ref[bb, :, sl] = hf[hd].astype(BF16)
            hb_ref[bb, :, sl] = hb[hd].astype(BF16)


def _scan(q, kt, v, gt, kt_c, v_c, gt_c, seq_len, ctx_len, nbb):
    n = q.shape[0]
    nb = n // seq_len
    nc = seq_len // CHUNK
    q3, v3 = q.reshape(nb, seq_len, M_W), v.reshape(nb, seq_len, M_W)
    fwd = lambda b, c: (b, c, 0)
    bwd = lambda b, c: (b, nc - 1 - c, 0)
    fwd_t = lambda b, c: (b, 0, c)
    bwd_t = lambda b, c: (b, 0, nc - 1 - c)
    tokb = lambda im: pl.BlockSpec((nbb, CHUNK, M_W), im)

    def side(tok_map, tok_map_t):
        return [tokb(tok_map), pl.BlockSpec((nbb, M_W, CHUNK), tok_map_t), tokb(tok_map),
                pl.BlockSpec((nbb, N_GATES, CHUNK), tok_map_t)]

    in_specs = (side(fwd, fwd_t) + side(bwd, bwd_t)
                + [pl.BlockSpec((nbb, M_W, ctx_len), lambda b, c: (b, 0, 0)),
                   pl.BlockSpec((nbb, ctx_len, M_W), lambda b, c: (b, 0, 0)),
                   pl.BlockSpec((nbb, N_GATES, ctx_len), lambda b, c: (b, 0, 0))])
    hf, hb = pl.pallas_call(
        functools.partial(_scan_kernel, ctx_chunks=ctx_len // CHUNK, nbb=nbb),
        grid=(nb // nbb, nc),
        in_specs=in_specs,
        out_specs=[tokb(fwd), tokb(bwd)],
        out_shape=[jax.ShapeDtypeStruct((nb, seq_len, M_W), BF16)] * 2,
        scratch_shapes=[pltpu.VMEM((nbb * 2 * M_HEADS, M_HEAD_DIM, 2 * M_HEAD_DIM), F32),
                        pltpu.VMEM((nbb * 2 * M_HEADS, 8, 128), F32)],
        compiler_params=_params(("parallel", "arbitrary")),
        name="mlstm_scan",
    )(q3, kt, v3, gt, q3, kt, v3, gt, kt_c, v_c.reshape(nb, ctx_len, M_W), gt_c)
    return hf.reshape(n, M_W), hb.reshape(n, M_W)


def _mixout_kernel(x_ref, mod_ref, p_ref, q_ref, hf_ref, hb_ref, a_ref, z_ref, mnw_ref, msk_ref,
                   wc_ref, wo_ref, n2w_ref, x1_ref, h2_ref):
    mod = mod_ref[0]
    gate1, shift2, scale2 = mod[2:3, :], mod[3:4, :], mod[4:5, :]
    yf = _dot(p_ref[...], wc_ref[0]) + _dot(q_ref[...], wc_ref[1])
    h = hf_ref[...].astype(F32) + hb_ref[...].astype(F32)
    parts = []
    for hd in range(M_HEADS):
        hh = h[:, hd * M_HEAD_DIM:(hd + 1) * M_HEAD_DIM]
        mu = jnp.mean(hh, axis=-1, keepdims=True)
        dlt = hh - mu
        var = jnp.mean(dlt * dlt, axis=-1, keepdims=True)
        parts.append(dlt * lax.rsqrt(var + NORM_EPS))
    hn = jnp.concatenate(parts, axis=-1) * mnw_ref[...]
    ym = (hn + msk_ref[...] * a_ref[...].astype(F32)) * _silu(z_ref[...].astype(F32))
    y = _dot(yf.astype(BF16), wo_ref[0]) + _dot(ym.astype(BF16), wo_ref[1])
    x1 = x_ref[...] + gate1 * y
    x1_ref[...] = x1
    h2_ref[...] = (_rms(x1, n2w_ref[...]) * (1.0 + scale2) + shift2).astype(BF16)


def _channel_dft():
    k = np.arange(F_GROUP_W, dtype=np.int64)
    ang = 2.0 * np.pi * ((k[:, None] * k[None, :]) % F_GROUP_W).astype(np.float64) / F_GROUP_W
    scale = 1.0 / math.sqrt(F_GROUP_W)
    eye = np.eye(F_GROUPS)
    wc = np.stack([np.kron(eye, np.cos(ang) * scale), -np.kron(eye, np.sin(ang) * scale)])
    return jnp.asarray(wc, dtype=F32).astype(BF16)


def _mixout(x2d, mod, p, q, hf, hb, a, z, mnorm_w, m_skip, w_out, norm2_w, seq_len, tm):
    n, d = x2d.shape
    tiles_per_seq = seq_len // tm
    tok = lambda i: (i, 0)
    half = pl.BlockSpec((tm, F_W), tok)
    full = pl.BlockSpec((tm, d), tok)
    vec = lambda w: pl.BlockSpec((1, w), lambda i: (0, 0))
    return pl.pallas_call(
        _mixout_kernel,
        grid=(n // tm,),
        in_specs=[full, pl.BlockSpec((1, 6, d), lambda i: (i // tiles_per_seq, 0, 0)),
                  half, half, half, half, half, half, vec(M_W), vec(M_W),
                  pl.BlockSpec((2, F_W, F_W), lambda i: (0, 0, 0)),
                  pl.BlockSpec((2, F_W, d), lambda i: (0, 0, 0)),
                  vec(d)],
        out_specs=[full, full],
        out_shape=[jax.ShapeDtypeStruct((n, d), F32), jax.ShapeDtypeStruct((n, d), BF16)],
        compiler_params=_params(("parallel",)),
        name="mixout",
    )(x2d, mod, p, q, hf, hb, a, z, mnorm_w.reshape(1, M_W), m_skip.reshape(1, M_W),
      _channel_dft(), w_out.astype(BF16).reshape(2, F_W, d), norm2_w.reshape(1, d))


def _ffn_kernel(h_ref, hp_ref, hn_ref, x1_ref, mod_ref, wu_ref, bu_ref, cw_ref, cb_ref, wd_ref, bd_ref,
                fw_ref, o_ref, hcat_scr, ua_scr, ub_scr, act_scr, *, rows, tiles, cw):
    t = pl.program_id(1)
    nt = rows * GRID_W
    nj = D_FF // cw
    hcat_scr[0:GRID_W] = hp_ref[...]
    hcat_scr[GRID_W:GRID_W + nt] = h_ref[...]
    hcat_scr[GRID_W + nt:] = hn_ref[...]
    row8 = lax.broadcasted_iota(jnp.int32, (8, 1), 0)
    o_ref[...] = jnp.zeros_like(o_ref)

    def up(j, u_scr):
        u = _dot(hcat_scr[...], wu_ref[j]) + bu_ref[j]
        for p in range(rows + 2):
            rs = slice(p * GRID_W, (p + 1) * GRID_W)
            piece = u[rs]
            if p == 0:
                piece = jnp.where(t > 0, piece, 0.0)
            if p == rows + 1:
                piece = jnp.where(t < tiles - 1, piece, 0.0)
            prev = pltpu.roll(piece, 1, 0)
            prev = jnp.concatenate([jnp.where(row8 == 0, 0.0, prev[0:8]), prev[8:]], axis=0)
            nxt = pltpu.roll(piece, GRID_W - 1, 0)
            nxt = jnp.concatenate([nxt[:GRID_W - 8], jnp.where(row8 == 7, 0.0, nxt[GRID_W - 8:])], axis=0)
            u_scr[0, rs] = prev.astype(CONV_DTYPE)
            u_scr[1, rs] = piece.astype(CONV_DTYPE)
            u_scr[2, rs] = nxt.astype(CONV_DTYPE)

    def consume(j, u_scr):
        groups = GRID_W // CONV_ROWS
        for r in range(rows):
            conv = cb_ref[j][None]
            for dr in range(3):
                rs = slice((r + dr) * GRID_W, (r + dr + 1) * GRID_W)
                for dc in range(3):
                    conv = conv + cw_ref[j, 3 * dr + dc][None] * u_scr[dc, rs].reshape(groups, CONV_ROWS, 2 * cw)
            conv = conv.reshape(GRID_W, 2 * cw)
            act_scr[r * GRID_W:(r + 1) * GRID_W, :] = (conv[:, :cw] * _silu(conv[:, cw:])).astype(BF16)
        o_ref[...] += _dot(act_scr[...], wd_ref[j])

    up(0, ua_scr)

    def body(jj, carry):
        j = 2 * jj
        up(j + 1, ub_scr)
        consume(j, ua_scr)
        up(j + 2, ua_scr)
        consume(j + 1, ub_scr)
        return carry

    for jj in range(nj // 2 - 1):
        body(jj, 0)
    up(nj - 1, ub_scr)
    consume(nj - 2, ua_scr)
    consume(nj - 1, ub_scr)
    y = o_ref[...] + bd_ref[...]
    o_ref[...] = _rms(x1_ref[...] + mod_ref[0][5:6, :] * y, fw_ref[...])


def _pair_cast_kernel(val_ref, gate_ref, o_ref, *, cw):
    o_ref[0, :, :cw] = val_ref[...].astype(BF16)
    o_ref[0, :, cw:] = gate_ref[...].astype(BF16)


def _pair_cast(w_up, cw):
    d = w_up.shape[0]
    nj = D_FF // cw
    return pl.pallas_call(
        functools.partial(_pair_cast_kernel, cw=cw),
        grid=(nj,),
        in_specs=[pl.BlockSpec((d, cw), lambda j: (0, j)), pl.BlockSpec((d, cw), lambda j: (0, nj + j))],
        out_specs=pl.BlockSpec((1, d, 2 * cw), lambda j: (j, 0, 0)),
        out_shape=jax.ShapeDtypeStruct((nj, d, 2 * cw), BF16),
        compiler_params=_params(("parallel",)),
        name="pair_cast",
    )(w_up, w_up)


def _ffn(h2, x1, mod, w_up, b_up, fconv_w, fconv_b, w_down, b_down, final_norm_w, seq_len, rows, cw):
    n, d = x1.shape
    nb = n // seq_len
    grid_h = seq_len // GRID_W
    tiles = grid_h // rows
    nt = rows * GRID_W
    nj = D_FF // cw

    def pair(w):
        lead = w.shape[:-1]
        w2 = w.reshape(lead + (2, nj, cw))
        w2 = jnp.moveaxis(w2, -2, 0)
        return w2.reshape((nj,) + lead + (2 * cw,))

    wu = _pair_cast(w_up, cw)
    bu = pair(b_up.reshape(1, 2 * D_FF))
    cwt = pair(fconv_w.reshape(9, 2 * D_FF)).astype(CONV_DTYPE)
    cwt = jnp.broadcast_to(cwt[:, :, None, :], (nj, 9, CONV_ROWS, 2 * cw))
    cbt = pair(fconv_b.reshape(1, 2 * D_FF)).astype(CONV_DTYPE)
    cbt = jnp.broadcast_to(cbt, (nj, CONV_ROWS, 2 * cw))
    wd = w_down.astype(BF16).reshape(nj, cw, d)
    nblk = n // GRID_W
    const3 = lambda b, t: (0, 0, 0)
    return pl.pallas_call(
        functools.partial(_ffn_kernel, rows=rows, tiles=tiles, cw=cw),
        grid=(nb, tiles),
        in_specs=[pl.BlockSpec((nt, d), lambda b, t: (b * tiles + t, 0)),
                  pl.BlockSpec((GRID_W, d), lambda b, t: (jnp.maximum((b * tiles + t) * rows - 1, 0), 0)),
                  pl.BlockSpec((GRID_W, d), lambda b, t: (jnp.minimum((b * tiles + t + 1) * rows, nblk - 1), 0)),
                  pl.BlockSpec((nt, d), lambda b, t: (b * tiles + t, 0)),
                  pl.BlockSpec((1, 6, d), lambda b, t: (b, 0, 0)),
                  pl.BlockSpec((nj, d, 2 * cw), const3, pipeline_mode=pl.Buffered(1)),
                  pl.BlockSpec((nj, 1, 2 * cw), const3),
                  pl.BlockSpec((nj, 9, CONV_ROWS, 2 * cw), lambda b, t: (0, 0, 0, 0), pipeline_mode=pl.Buffered(1)),
                  pl.BlockSpec((nj, CONV_ROWS, 2 * cw), const3),
                  pl.BlockSpec((nj, cw, d), const3, pipeline_mode=pl.Buffered(1)),
                  pl.BlockSpec((1, d), lambda b, t: (0, 0)),
                  pl.BlockSpec((1, d), lambda b, t: (0, 0))],
        out_specs=pl.BlockSpec((nt, d), lambda b, t: (b * tiles + t, 0)),
        out_shape=jax.ShapeDtypeStruct((n, d), F32),
        scratch_shapes=[pltpu.VMEM((nt + 2 * GRID_W, d), BF16),
                        pltpu.VMEM((3, nt + 2 * GRID_W, 2 * cw), CONV_DTYPE),
                        pltpu.VMEM((3, nt + 2 * GRID_W, 2 * cw), CONV_DTYPE),
                        pltpu.VMEM((nt, cw), BF16)],
        compiler_params=_params(("parallel", "parallel")),
        name="conv_ffn",
    )(h2, h2, h2, x1, mod, wu, bu, cwt, cbt, wd, b_down.reshape(1, d), final_norm_w.reshape(1, d))


def kernel(x, c, ctx, c_ctx, w_ada, b_ada, norm1_w, w_in, mconv_w, mconv_b, w_q, w_k, b_gate, mnorm_w, m_skip,
           w_out, norm2_w, w_up, b_up, fconv_w, fconv_b, w_down, b_down, final_norm_w):
    bsz, seq_len, d = x.shape
    ctx_len = ctx.shape[1]
    assert w_ada.shape[0] == 1, "single-layer kernel"
    cond = jnp.concatenate([c, c_ctx[None, :], jnp.zeros((16 - bsz - 1, d), F32)], axis=0)
    mod = _adaln(cond, w_ada[0], b_ada[0]).reshape(16, 6, d)

    x2d = x.reshape(bsz * seq_len, d)
    ctx2d = ctx.reshape(bsz * ctx_len, d)

    xm_c, v_c, gt_c = _inproj(ctx2d, mod, lambda i: bsz, norm1_w[0], w_in[0], b_gate[0],
                              ctx_len, ctx_len, False)
    _, _, kt_c = _prep(xm_c, mconv_w[0], mconv_b[0], w_q[0], w_k[0], ctx_len, ctx_len)

    xm, v, z, gt, u_perm = _inproj(x2d, mod, lambda i: i // (seq_len // INPROJ_TILE), norm1_w[0], w_in[0],
                                   b_gate[0], seq_len, INPROJ_TILE, True)
    p, q_im = _fourier(u_perm, seq_len)
    a, q, kt = _prep(xm, mconv_w[0], mconv_b[0], w_q[0], w_k[0], seq_len, PREP_TILE)
    hf, hb = _scan(q, kt, v, gt, kt_c, v_c, gt_c, seq_len, ctx_len, SCAN_SEQS)
    x1, h2 = _mixout(x2d, mod, p, q_im, hf, hb, a, z, mnorm_w[0], m_skip[0], w_out[0], norm2_w[0],
                     seq_len, MIXOUT_TILE)
    out = _ffn(h2, x1, mod, w_up[0], b_up[0], fconv_w[0], fconv_b[0], w_down[0], b_down[0],
               final_norm_w, seq_len, FFN_GRID_ROWS, FFN_CHANNELS)
    return out.reshape(bsz, seq_len, d)
```
